```python
import jax, jax.numpy as jnp
from jax import lax
import numpy as np

D_MODEL = 2048
BATCH = 4
SEQ = 8192
DEPTH = 1

D_MIX = D_MODEL
D_CONV = D_MIX // 2
CONV_WIDTH = 3
N_HEADS = 8
HEAD_DIM = 128
D_ATTN = N_HEADS * HEAD_DIM
N_KV_HEADS = 2
GQA_GROUP = N_HEADS // N_KV_HEADS
D_KV = N_KV_HEADS * HEAD_DIM
N_BRANCH = 3
CMP_LEN = 32
CMP_STRIDE = 16
CMP_HIDDEN = 256
SLC_LEN = 64
N_SLC = 16
WINDOW = 512
Q_BLOCK = 128
ROPE_THETA = 10000.0
D_FF = 4 * D_MODEL
EPS = 1e-6
FORCE_BONUS = 1e4
D_IN = 3 * D_CONV + D_ATTN + 2 * N_BRANCH * D_KV + N_BRANCH * N_HEADS

kernel_name = "hybrid_shortconv_nsa_adaln_block"


def rms_norm(x, g):
    xf = x.astype(jnp.float32)
    y = xf * lax.rsqrt(jnp.mean(xf * xf, axis=-1, keepdims=True) + EPS)
    return (y * g.astype(jnp.float32)).astype(x.dtype)


def modulate(h, shift, scale):
    return h * (1 + scale[:, None, :]) + shift[:, None, :]


def rope_tables(seq):
    inv = ROPE_THETA ** (-jnp.arange(0, HEAD_DIM, 2, dtype=jnp.float32) / HEAD_DIM)
    ang = jnp.arange(seq, dtype=jnp.float32)[:, None] * inv[None, :]
    return jnp.cos(ang), jnp.sin(ang)


def apply_rope(x, cos, sin):
    xf = x.astype(jnp.float32)
    x1, x2 = jnp.split(xf, 2, axis=-1)
    c = cos[:, None, :]
    s = sin[:, None, :]
    return jnp.concatenate([x1 * c - x2 * s, x2 * c + x1 * s], axis=-1).astype(x.dtype)


def masked_softmax(s, mask):
    s = jnp.where(mask, s, jnp.finfo(jnp.float32).min)
    p = jax.nn.softmax(s, axis=-1)
    return jnp.where(mask, p, 0.0)


def split_points():
    sizes = [D_CONV, D_CONV, D_CONV, D_ATTN] + [D_KV] * (2 * N_BRANCH) + [N_BRANCH * N_HEADS]
    return [int(v) for v in np.cumsum(sizes)[:-1]]


def short_conv_mixer(u_b, u_c, u_h, conv_w, conv_b):
    v = u_c * u_h
    S = v.shape[1]
    vp = jnp.pad(v, ((0, 0), (CONV_WIDTH - 1, 0), (0, 0)))
    z = conv_b + sum(conv_w[k] * vp[:, k:k + S] for k in range(CONV_WIDTH))
    return u_b * z


def compress(kv, pe, w1, w2):
    S = kv.shape[2]
    n_cmp = (S - CMP_LEN) // CMP_STRIDE + 1
    idx = jnp.arange(n_cmp)[:, None] * CMP_STRIDE + jnp.arange(CMP_LEN)[None, :]
    blocks = kv[:, :, idx, :] + pe
    flat = blocks.reshape(blocks.shape[:3] + (CMP_LEN * HEAD_DIM,))
    return jax.nn.gelu(flat @ w1) @ w2


def nsa_attention(q, k_cmp, v_cmp, k_slc, v_slc, k_win, v_win, gates):
    B, _, S, _ = q.shape
    n_cmp = k_cmp.shape[2]
    n_slc = S // SLC_LEN
    top = min(N_SLC, n_slc)
    n_qb = S // Q_BLOCK
    scale = HEAD_DIM ** -0.5
    cmp_end = jnp.arange(n_cmp) * CMP_STRIDE + CMP_LEN - 1
    ci = jnp.arange(n_cmp)[:, None]
    sj = jnp.arange(n_slc)[None, :]
    cmp_to_slc = ((ci * CMP_STRIDE <= sj * SLC_LEN + SLC_LEN - 1)
                  & (ci * CMP_STRIDE + CMP_LEN - 1 >= sj * SLC_LEN)).astype(jnp.float32)
    k_blocks = k_slc.reshape(B, N_KV_HEADS, n_slc, SLC_LEN, HEAD_DIM)
    v_blocks = v_slc.reshape(B, N_KV_HEADS, n_slc, SLC_LEN, HEAD_DIM)
    pad = ((0, 0), (0, 0), (WINDOW, 0), (0, 0))
    k_wp = jnp.pad(k_win, pad)
    v_wp = jnp.pad(v_win, pad)
    bi = jnp.arange(B)[:, None, None, None]
    gi = jnp.arange(N_KV_HEADS)[None, :, None, None]
    blk_ids = jnp.arange(n_slc)[None, :]
    in_blk = jnp.arange(SLC_LEN)

    def one_block(qb):
        qs = qb * Q_BLOCK
        t = qs + jnp.arange(Q_BLOCK)
        qbk = lax.dynamic_slice_in_dim(q, qs, Q_BLOCK, axis=2).reshape(
            B, N_KV_HEADS, GQA_GROUP, Q_BLOCK, HEAD_DIM)
        s = jnp.einsum('bgrqd,bgnd->bgrqn', qbk, k_cmp).astype(jnp.float32) * scale
        p_cmp = masked_softmax(s, cmp_end[None, :] <= t[:, None])
        o_cmp = jnp.einsum('bgrqn,bgnd->bgrqd', p_cmp.astype(v_cmp.dtype), v_cmp)
        imp = jnp.einsum('bgrqn,nj->bgqj', p_cmp, cmp_to_slc)
        cur = (t // SLC_LEN)[:, None]
        valid = blk_ids * SLC_LEN <= t[:, None]
        forced = (blk_ids == 0) | (blk_ids == cur) | (blk_ids == cur - 1)
        score = jnp.where(valid, imp + jnp.where(forced, FORCE_BONUS, 0.0), -1.0)
        _, sel = lax.top_k(score, top)
        k_sel = k_blocks[bi, gi, sel].reshape(B, N_KV_HEADS, Q_BLOCK, top * SLC_LEN, HEAD_DIM)
        v_sel = v_blocks[bi, gi, sel].reshape(B, N_KV_HEADS, Q_BLOCK, top * SLC_LEN, HEAD_DIM)
        pos = (sel[..., None] * SLC_LEN + in_blk).reshape(B, N_KV_HEADS, Q_BLOCK, top * SLC_LEN)
        s = jnp.einsum('bgrqd,bgqkd->bgrqk', qbk, k_sel).astype(jnp.float32) * scale
        p = masked_softmax(s, (pos <= t[:, None])[:, :, None])
        o_slc = jnp.einsum('bgrqk,bgqkd->bgrqd', p.astype(v_sel.dtype), v_sel)
        kw = lax.dynamic_slice_in_dim(k_wp, qs, WINDOW + Q_BLOCK, axis=2)
        vw = lax.dynamic_slice_in_dim(v_wp, qs, WINDOW + Q_BLOCK, axis=2)
        kpos = (qs - WINDOW + jnp.arange(WINDOW + Q_BLOCK))[None, :]
        wmask = (kpos >= 0) & (kpos <= t[:, None]) & (kpos > t[:, None] - WINDOW)
        s = jnp.einsum('bgrqd,bgkd->bgrqk', qbk, kw).astype(jnp.float32) * scale
        p = masked_softmax(s, wmask)
        o_win = jnp.einsum('bgrqk,bgkd->bgrqd', p.astype(vw.dtype), vw)
        g = lax.dynamic_slice_in_dim(gates, qs, Q_BLOCK, axis=1).reshape(
            B, Q_BLOCK, N_KV_HEADS, GQA_GROUP, N_BRANCH).transpose(0, 2, 3, 1, 4)
        return g[..., 0:1] * o_cmp + g[..., 1:2] * o_slc + g[..., 2:3] * o_win

    out = lax.map(one_block, jnp.arange(n_qb))
    return out.transpose(1, 0, 4, 2, 3, 5).reshape(B, S, D_ATTN)


def setup_inputs(seed: int = 0) -> dict:
    key = jax.random.key(seed)
    ks = jax.random.split(key, 24)
    f32 = jnp.float32

    def nrm(k, shape, s):
        return jax.random.normal(k, shape, f32) * s

    L = DEPTH
    return {
        "x": nrm(ks[0], (BATCH, SEQ, D_MODEL), 1.0),
        "c": nrm(ks[1], (BATCH, D_MODEL), 1.0),
        "w_ada": nrm(ks[2], (L, D_MODEL, 6 * D_MODEL), 0.5 * D_MODEL ** -0.5),
        "b_ada": nrm(ks[3], (L, 6 * D_MODEL), 0.02),
        "norm1_g": 1.0 + nrm(ks[4], (L, D_MODEL), 0.02),
        "w_in": nrm(ks[5], (L, D_MODEL, D_IN), D_MODEL ** -0.5),
        "conv_w": nrm(ks[6], (L, CONV_WIDTH, D_CONV), CONV_WIDTH ** -0.5),
        "conv_b": nrm(ks[7], (L, D_CONV), 0.02),
        "cmp_pe_k": nrm(ks[8], (L, CMP_LEN, HEAD_DIM), 0.1),
        "cmp_pe_v": nrm(ks[9], (L, CMP_LEN, HEAD_DIM), 0.1),
        "cmp_w1_k": nrm(ks[10], (L, CMP_LEN * HEAD_DIM, CMP_HIDDEN), (CMP_LEN * HEAD_DIM) ** -0.5),
        "cmp_w2_k": nrm(ks[11], (L, CMP_HIDDEN, HEAD_DIM), CMP_HIDDEN ** -0.5),
        "cmp_w1_v": nrm(ks[12], (L, CMP_LEN * HEAD_DIM, CMP_HIDDEN), (CMP_LEN * HEAD_DIM) ** -0.5),
        "cmp_w2_v": nrm(ks[13], (L, CMP_HIDDEN, HEAD_DIM), CMP_HIDDEN ** -0.5),
        "gnorm_conv_g": 1.0 + nrm(ks[14], (L, D_CONV), 0.02),
        "gnorm_attn_g": 1.0 + nrm(ks[15], (L, D_ATTN), 0.02),
        "w_out": nrm(ks[16], (L, D_MIX, D_MODEL), D_MIX ** -0.5),
        "norm2_g": 1.0 + nrm(ks[17], (L, D_MODEL), 0.02),
        "w_ff1": nrm(ks[18], (L, D_MODEL, D_FF), D_MODEL ** -0.5),
        "w_ff2": nrm(ks[19], (L, D_FF, D_MODEL), D_FF ** -0.5),
        "normf_g": 1.0 + nrm(ks[20], (D_MODEL,), 0.02),
    }


def reference(x, c, w_ada, b_ada, norm1_g, w_in, conv_w, conv_b, cmp_pe_k, cmp_pe_v,
              cmp_w1_k, cmp_w2_k, cmp_w1_v, cmp_w2_v, gnorm_conv_g, gnorm_attn_g,
              w_out, norm2_g, w_ff1, w_ff2, normf_g):
    B, S, _ = x.shape
    cos, sin = rope_tables(S)
    mod_all = jnp.einsum('bd,ldm->lbm', jax.nn.silu(c), w_ada) + b_ada[:, None, :]
    cuts = split_points()

    def kv_heads(t, rope):
        t = t.reshape(B, S, N_KV_HEADS, HEAD_DIM)
        if rope:
            t = apply_rope(t, cos, sin)
        return t.transpose(0, 2, 1, 3)

    for l in range(DEPTH):
        sh1, sc1, g1, sh2, sc2, g2 = jnp.split(mod_all[l], 6, axis=-1)
        h = modulate(rms_norm(x, norm1_g[l]), sh1, sc1)
        u = h @ w_in[l]
        u_b, u_c, u_h, q, kc, vc, ksl, vsl, kwn, vwn, gl = jnp.split(u, cuts, axis=-1)
        y_conv = short_conv_mixer(u_b, u_c, u_h, conv_w[l], conv_b[l])
        q = apply_rope(q.reshape(B, S, N_HEADS, HEAD_DIM), cos, sin).transpose(0, 2, 1, 3)
        k_cmp = compress(kv_heads(kc, True), cmp_pe_k[l], cmp_w1_k[l], cmp_w2_k[l])
        v_cmp = compress(kv_heads(vc, False), cmp_pe_v[l], cmp_w1_v[l], cmp_w2_v[l])
        gates = jax.nn.sigmoid(gl.astype(jnp.float32)).astype(x.dtype).reshape(
            B, S, N_HEADS, N_BRANCH)
        y_attn = nsa_attention(q, k_cmp, v_cmp, kv_heads(ksl, True), kv_heads(vsl, False),
                               kv_heads(kwn, True), kv_heads(vwn, False), gates)
        mixed = jnp.concatenate([rms_norm(y_conv, gnorm_conv_g[l]),
                                 rms_norm(y_attn, gnorm_attn_g[l])], axis=-1)
        x = x + g1[:, None, :] * (mixed @ w_out[l])
        h = modulate(rms_norm(x, norm2_g[l]), sh2, sc2)
        x = x + g2[:, None, :] * (jnp.square(jax.nn.relu(h @ w_ff1[l])) @ w_ff2[l])
    return rms_norm(x, normf_g)
```

```python
import functools

import numpy as np
import jax
import jax.numpy as jnp
from jax import lax
from jax.experimental import pallas as pl
from jax.experimental.pallas import tpu as pltpu

F32 = jnp.float32
BF16 = jnp.bfloat16

HEAD_DIM = 128
N_HEADS = 8
N_KV_HEADS = 2
GQA_GROUP = N_HEADS // N_KV_HEADS
D_CONV = 1024
D_ATTN = N_HEADS * HEAD_DIM
D_KV = N_KV_HEADS * HEAD_DIM
N_BRANCH = 3
CONV_WIDTH = 3
CMP_LEN = 32
CMP_STRIDE = 16
CMP_HIDDEN = 256
SLC_LEN = 64
N_SLC = 16
WINDOW = 512
ROPE_THETA = 10000.0
EPS = 1e-6
FORCE_BONUS = 1e4

LANES = 128
SUBLANES = 8
VMEM_LIMIT_BYTES = 56 * 1024 * 1024

NEG = float(np.finfo(np.float32).min)

COL_UB, COL_UC, COL_UH = 0, D_CONV, 2 * D_CONV
COL_Q = 3 * D_CONV
COL_KV = COL_Q + D_ATTN
COL_GATE = COL_KV + 2 * N_BRANCH * D_KV
D_IN = COL_GATE + N_BRANCH * N_HEADS
D_IN_PAD = COL_GATE + LANES

TM_PROJ = 512
TQ = 128
TK_SLC = 512
TM_FFN = 512
TF_FFN = 1024
TN_ADA = 1024


def _params(sem):
    return pltpu.CompilerParams(dimension_semantics=sem, vmem_limit_bytes=VMEM_LIMIT_BYTES)


def _dot(a, b):
    return jnp.dot(a, b, preferred_element_type=F32)


def _dot_nt(a, b):
    return lax.dot_general(a, b, (((1,), (1,)), ((), ())), preferred_element_type=F32)


def _rms(x, g):
    return x * lax.rsqrt(jnp.mean(x * x, axis=-1, keepdims=True) + EPS) * g


def _adaln_kernel(c_ref, w_ref, b_ref, o_ref):
    c = c_ref[...]
    s = c * jax.nn.sigmoid(c)
    o_ref[...] = _dot(s.astype(BF16), w_ref[...].astype(BF16)) + b_ref[...]


def _adaln(c, w_ada, b_ada):
    B, D = c.shape
    n_out = w_ada.shape[1]
    rows = -(-B // SUBLANES) * SUBLANES
    c_pad = jnp.zeros((rows, D), F32).at[:B].set(c)
    out = pl.pallas_call(
        _adaln_kernel,
        grid=(n_out // TN_ADA,),
        in_specs=[
            pl.BlockSpec((rows, D), lambda j: (0, 0)),
            pl.BlockSpec((D, TN_ADA), lambda j: (0, j)),
            pl.BlockSpec((1, TN_ADA), lambda j: (0, j)),
        ],
        out_specs=pl.BlockSpec((rows, TN_ADA), lambda j: (0, j)),
        out_shape=jax.ShapeDtypeStruct((rows, n_out), F32),
        compiler_params=_params(("arbitrary",)),
        name="adaln",
    )(c_pad, w_ada, b_ada.reshape(1, n_out))
    return out[:B]


def _rope(u, cos_f, sin_f):
    return u * cos_f + pltpu.roll(u, HEAD_DIM // 2, 1) * sin_f


def _in_proj_kernel(x_ref, sh_ref, sc_ref, g_ref, w_ref, cw_ref, cb_ref, gc_ref, cos_ref, sin_ref,
                    yconv_ref, q_ref, kc_ref, vc_ref, ksl_ref, vsl_ref, kwn_ref, vwn_ref, gate_ref,
                    vbuf_ref, *, tm, q_scale):
    i = pl.program_id(1)
    x = x_ref[...]
    h = _rms(x, g_ref[...]) * (1.0 + sc_ref[...]) + sh_ref[...]
    hb = h.astype(BF16)

    ub = _dot(hb, w_ref[:, COL_UB:COL_UB + D_CONV])
    uc = _dot(hb, w_ref[:, COL_UC:COL_UC + D_CONV])
    uh = _dot(hb, w_ref[:, COL_UH:COL_UH + D_CONV])
    v = uc * uh

    @pl.when(i == 0)
    def _():
        vbuf_ref[0:SUBLANES, :] = jnp.zeros((SUBLANES, D_CONV), F32)

    vbuf_ref[SUBLANES:SUBLANES + tm, :] = v
    v1 = vbuf_ref[SUBLANES - 1:SUBLANES - 1 + tm, :]
    v2 = vbuf_ref[SUBLANES - 2:SUBLANES - 2 + tm, :]
    z = cb_ref[...] + cw_ref[0:1, :] * v2 + cw_ref[1:2, :] * v1 + cw_ref[2:3, :] * v
    vbuf_ref[0:SUBLANES, :] = vbuf_ref[tm:tm + SUBLANES, :]
    yconv_ref[...] = _rms(ub * z, gc_ref[...]).astype(BF16)

    cos_f = cos_ref[...]
    sin_f = sin_ref[...]

    uq = _dot(hb, w_ref[:, COL_Q:COL_Q + D_ATTN])
    for hd in range(N_HEADS):
        sl = slice(hd * HEAD_DIM, (hd + 1) * HEAD_DIM)
        q_ref[:, sl] = (_rope(uq[:, sl], cos_f, sin_f) * q_scale).astype(BF16)

    ukv = _dot(hb, w_ref[:, COL_KV:COL_GATE])
    outs = (kc_ref, vc_ref, ksl_ref, vsl_ref, kwn_ref, vwn_ref)
    for n, o_ref in enumerate(outs):
        for g in range(N_KV_HEADS):
            src = slice(n * D_KV + g * HEAD_DIM, n * D_KV + (g + 1) * HEAD_DIM)
            dst = slice(g * HEAD_DIM, (g + 1) * HEAD_DIM)
            t = ukv[:, src]
            if n % 2 == 0:
                t = _rope(t, cos_f, sin_f)
            o_ref[:, dst] = t.astype(o_ref.dtype)

    ug = _dot(hb, w_ref[:, COL_GATE:D_IN_PAD])
    gate_ref[...] = jax.nn.sigmoid(ug)


def _in_proj(x, mod, norm1_g, w_in_b, conv_w, conv_b, gconv_g, cos_f, sin_f):
    B, S, D = x.shape
    tm = min(TM_PROJ, S)
    tok = lambda width: pl.BlockSpec((None, tm, width), lambda b, i: (b, i, 0))
    vec = lambda width: pl.BlockSpec((1, width), lambda b, i: (0, 0))
    modspec = lambda k: pl.BlockSpec((None, None, 1, D), lambda b, i, k=k: (b, k, 0, 0))
    kernel = functools.partial(_in_proj_kernel, tm=tm, q_scale=HEAD_DIM ** -0.5)
    sd = jax.ShapeDtypeStruct
    return pl.pallas_call(
        kernel,
        grid=(B, S // tm),
        in_specs=[
            tok(D), modspec(0), modspec(1), vec(D),
            pl.BlockSpec((D, D_IN_PAD), lambda b, i: (0, 0), pipeline_mode=pl.Buffered(1)),
            pl.BlockSpec((CONV_WIDTH, D_CONV), lambda b, i: (0, 0)), vec(D_CONV), vec(D_CONV),
            pl.BlockSpec((tm, HEAD_DIM), lambda b, i: (i, 0)),
            pl.BlockSpec((tm, HEAD_DIM), lambda b, i: (i, 0)),
        ],
        out_specs=[tok(D_CONV), tok(D_ATTN), tok(D_KV), tok(D_KV), tok(D_KV), tok(D_KV), tok(D_KV),
                   tok(D_KV), tok(LANES)],
        out_shape=[sd((B, S, D_CONV), BF16), sd((B, S, D_ATTN), BF16),
                   sd((B, S, D_KV), F32), sd((B, S, D_KV), F32),
                   sd((B, S, D_KV), BF16), sd((B, S, D_KV), BF16),
                   sd((B, S, D_KV), BF16), sd((B, S, D_KV), BF16),
                   sd((B, S, LANES), F32)],
        scratch_shapes=[pltpu.VMEM((tm + 2 * SUBLANES, D_CONV), F32)],
        compiler_params=_params(("arbitrary", "arbitrary")),
        name="in_proj",
    )(x, mod, mod, norm1_g, w_in_b, conv_w, conv_b, gconv_g, cos_f, sin_f)


def _compress_kernel(kv_ref, pe_ref, w1_ref, w2_ref, o_ref, buf_ref, *, seq, n_blk):
    buf_ref[0:seq, :] = kv_ref[...]
    buf_ref[seq:seq + CMP_STRIDE, :] = jnp.zeros((CMP_STRIDE, HEAD_DIM), F32)
    acc = jnp.zeros((n_blk, CMP_HIDDEN), F32)
    for l in range(CMP_LEN):
        rows = buf_ref[pl.ds(l, n_blk, stride=CMP_STRIDE), :] + pe_ref[l:l + 1, :]
        acc = acc + _dot(rows.astype(BF16), w1_ref[l * HEAD_DIM:(l + 1) * HEAD_DIM, :])
    hid = jax.nn.gelu(acc)
    o_ref[...] = _dot(hid.astype(BF16), w2_ref[...]).astype(BF16)


def _compress(kv, pe, w1_b, w2_b):
    B, S, _ = kv.shape
    n_blk = S // CMP_STRIDE
    kernel = functools.partial(_compress_kernel, seq=S, n_blk=n_blk)
    return pl.pallas_call(
        kernel,
        grid=(B, N_KV_HEADS),
        in_specs=[
            pl.BlockSpec((None, S, HEAD_DIM), lambda b, g: (b, 0, g)),
            pl.BlockSpec((CMP_LEN, HEAD_DIM), lambda b, g: (0, 0)),
            pl.BlockSpec((CMP_LEN * HEAD_DIM, CMP_HIDDEN), lambda b, g: (0, 0)),
            pl.BlockSpec((CMP_HIDDEN, HEAD_DIM), lambda b, g: (0, 0)),
        ],
        out_specs=pl.BlockSpec((None, None, n_blk, HEAD_DIM), lambda b, g: (b, g, 0, 0)),
        out_shape=jax.ShapeDtypeStruct((B, N_KV_HEADS, n_blk, HEAD_DIM), BF16),
        scratch_shapes=[pltpu.VMEM((S + CMP_STRIDE, HEAD_DIM), F32)],
        compiler_params=_params(("arbitrary", "arbitrary")),
        name="compress",
    )(kv, pe, w1_b, w2_b)


def _split3_dot(p, m01):
    hi = p.astype(BF16)
    r1 = p - hi.astype(F32)
    mid = r1.astype(BF16)
    lo = (r1 - mid.astype(F32)).astype(BF16)
    return _dot(hi, m01) + _dot(mid, m01) + _dot(lo, m01)


def _stack_heads(q_ref):
    return jnp.concatenate([q_ref[:, r * HEAD_DIM:(r + 1) * HEAD_DIM] for r in range(GQA_GROUP)], axis=0)


def _gate_col(gates, g, r, branch):
    lo = r * N_BRANCH + branch
    hi = (GQA_GROUP + r) * N_BRANCH + branch
    return jnp.where(g == 0, gates[:, lo:lo + 1], gates[:, hi:hi + 1])


def _cmp_select_kernel(q_ref, kc_ref, vc_ref, c2s_ref, gate_ref, ocmp_ref, sel_ref, *, tq, n_blk):
    g = pl.program_id(1)
    qs = pl.program_id(2) * tq
    rows = GQA_GROUP * tq
    qst = _stack_heads(q_ref)
    s = _dot_nt(qst, kc_ref[...])
    t_row = qs + (lax.broadcasted_iota(jnp.int32, (rows, n_blk), 0) & (tq - 1))
    cmp_end = lax.broadcasted_iota(jnp.int32, (rows, n_blk), 1) * CMP_STRIDE + (CMP_LEN - 1)
    mask = cmp_end <= t_row
    s = jnp.where(mask, s, NEG)
    m = jnp.max(s, axis=-1, keepdims=True)
    p = jnp.where(mask, jnp.exp(s - m), 0.0)
    l = jnp.sum(p, axis=-1, keepdims=True)
    p = p / jnp.where(l > 0.0, l, 1.0)
    o = _dot(p.astype(BF16), vc_ref[...])
    gates = gate_ref[...]
    for r in range(GQA_GROUP):
        ocmp_ref[:, r * HEAD_DIM:(r + 1) * HEAD_DIM] = _gate_col(gates, g, r, 0) * o[r * tq:(r + 1) * tq, :]

    psum = p[0:tq] + p[tq:2 * tq] + p[2 * tq:3 * tq] + p[3 * tq:4 * tq]
    imp = _split3_dot(psum, c2s_ref[...])
    t = qs + lax.broadcasted_iota(jnp.int32, (tq, LANES), 0)
    blk = lax.broadcasted_iota(jnp.int32, (tq, LANES), 1)
    cur = t // SLC_LEN
    valid = blk * SLC_LEN <= t
    forced = (blk == 0) | (blk == cur) | (blk == cur - 1)
    score = jnp.where(valid, imp + jnp.where(forced, FORCE_BONUS, 0.0), -1.0)
    lane = blk.astype(F32)
    chosen = jnp.zeros((tq, LANES), F32)
    for _ in range(N_SLC):
        mx = jnp.max(score, axis=-1, keepdims=True)
        first = jnp.min(jnp.where(score == mx, lane, float(LANES)), axis=-1, keepdims=True)
        hit = lane == first
        chosen = jnp.where(hit, 1.0, chosen)
        score = jnp.where(hit, -jnp.inf, score)
    sel_ref[...] = jnp.where(valid, chosen, 0.0).astype(BF16)


def _cmp_select(q, k_cmp, v_cmp, c2s, gates):
    B, S, _ = q.shape
    n_blk = k_cmp.shape[2]
    tq = min(TQ, S)
    kernel = functools.partial(_cmp_select_kernel, tq=tq, n_blk=n_blk)
    grp = GQA_GROUP * HEAD_DIM
    return pl.pallas_call(
        kernel,
        grid=(B, N_KV_HEADS, S // tq),
        in_specs=[
            pl.BlockSpec((None, tq, grp), lambda b, g, i: (b, i, g)),
            pl.BlockSpec((None, None, n_blk, HEAD_DIM), lambda b, g, i: (b, g, 0, 0)),
            pl.BlockSpec((None, None, n_blk, HEAD_DIM), lambda b, g, i: (b, g, 0, 0)),
            pl.BlockSpec((n_blk, LANES), lambda b, g, i: (0, 0)),
            pl.BlockSpec((None, tq, LANES), lambda b, g, i: (b, i, 0)),
        ],
        out_specs=[
            pl.BlockSpec((None, tq, grp), lambda b, g, i: (b, i, g)),
            pl.BlockSpec((None, None, tq, LANES), lambda b, g, i: (b, g, i, 0)),
        ],
        out_shape=[jax.ShapeDtypeStruct((B, S, D_ATTN), F32),
                   jax.ShapeDtypeStruct((B, N_KV_HEADS, S, LANES), BF16)],
        compiler_params=_params(("arbitrary", "arbitrary", "arbitrary")),
        name="cmp_select",
    )(q, k_cmp, v_cmp, c2s, gates)


def _slc_win_kernel(q_ref, ksl_ref, vsl_ref, kwn_ref, vwn_ref, sel_ref, exp_ref, gate_ref, ocmp_ref,
                    y_ref, *, tq, tk):
    g = pl.program_id(1)
    qs = pl.program_id(2) * tq
    rows = GQA_GROUP * tq
    qst = _stack_heads(q_ref)
    sel = sel_ref[...]

    def body(kt, carry):
        m, l, acc = carry
        k0 = pl.multiple_of(kt * tk, tk)
        s = _dot_nt(qst, ksl_ref[pl.ds(k0, tk), :])
        picked = _dot(sel, exp_ref[:, pl.ds(k0, tk)])
        t = qs + lax.broadcasted_iota(jnp.int32, (tq, tk), 0)
        kpos = k0 + lax.broadcasted_iota(jnp.int32, (tq, tk), 1)
        ok = jnp.where(kpos <= t, picked, 0.0)
        ok4 = jnp.concatenate([ok] * GQA_GROUP, axis=0) > 0.5
        s = jnp.where(ok4, s, NEG)
        m_new = jnp.maximum(m, jnp.max(s, axis=-1, keepdims=True))
        p = jnp.where(ok4, jnp.exp(s - m_new), 0.0)
        alpha = jnp.exp(m - m_new)
        l = alpha * l + jnp.sum(p, axis=-1, keepdims=True)
        acc = alpha * acc + _dot(p.astype(BF16), vsl_ref[pl.ds(k0, tk), :])
        return m_new, l, acc

    n_kt = (qs + tq + tk - 1) // tk
    init = (jnp.full((rows, 1), NEG, F32), jnp.zeros((rows, 1), F32), jnp.zeros((rows, HEAD_DIM), F32))
    _, l_s, acc_s = lax.fori_loop(0, n_kt, body, init)
    o_slc = acc_s / jnp.where(l_s > 0.0, l_s, 1.0)

    t4 = qs + (lax.broadcasted_iota(jnp.int32, (rows, tq), 0) & (tq - 1))
    col = lax.broadcasted_iota(jnp.int32, (rows, tq), 1)
    n_chunk = WINDOW // tq + 1
    s_parts, ok_parts, v_parts = [], [], []
    for c in range(n_chunk):
        start = qs - WINDOW + c * tq
        src = pl.multiple_of(jnp.maximum(start, 0), tq)
        kpos = start + col
        ok = (kpos >= 0) & (kpos <= t4) & (kpos > t4 - WINDOW)
        s_c = _dot_nt(qst, kwn_ref[pl.ds(src, tq), :])
        s_parts.append(jnp.where(ok, s_c, NEG))
        ok_parts.append(ok)
        v_parts.append(vwn_ref[pl.ds(src, tq), :])
    m_w = functools.reduce(jnp.maximum, [jnp.max(sp, axis=-1, keepdims=True) for sp in s_parts])
    l_w = jnp.zeros((rows, 1), F32)
    acc_w = jnp.zeros((rows, HEAD_DIM), F32)
    for sp, ok, vp in zip(s_parts, ok_parts, v_parts):
        p = jnp.where(ok, jnp.exp(sp - m_w), 0.0)
        l_w = l_w + jnp.sum(p, axis=-1, keepdims=True)
        acc_w = acc_w + _dot(p.astype(BF16), vp)
    o_win = acc_w / jnp.where(l_w > 0.0, l_w, 1.0)

    gates = gate_ref[...]
    for r in range(GQA_GROUP):
        rs = slice(r * tq, (r + 1) * tq)
        cs = slice(r * HEAD_DIM, (r + 1) * HEAD_DIM)
        y_ref[:, cs] = (ocmp_ref[:, cs] + _gate_col(gates, g, r, 1) * o_slc[rs, :]
                        + _gate_col(gates, g, r, 2) * o_win[rs, :])


def _slc_win(q, ksl, vsl, kwn, vwn, sel, expand, gates, ocmp):
    B, S, _ = q.shape
    tq = min(TQ, S)
    tk = min(TK_SLC, S)
    grp = GQA_GROUP * HEAD_DIM
    kernel = functools.partial(_slc_win_kernel, tq=tq, tk=tk)
    kv = pl.BlockSpec((None, S, HEAD_DIM), lambda b, g, i: (b, 0, g))
    return pl.pallas_call(
        kernel,
        grid=(B, N_KV_HEADS, S // tq),
        in_specs=[
            pl.BlockSpec((None, tq, grp), lambda b, g, i: (b, i, g)),
            kv, kv, kv, kv,
            pl.BlockSpec((None, None, tq, LANES), lambda b, g, i: (b, g, i, 0)),
            pl.BlockSpec((LANES, S), lambda b, g, i: (0, 0)),
            pl.BlockSpec((None, tq, LANES), lambda b, g, i: (b, i, 0)),
            pl.BlockSpec((None, tq, grp), lambda b, g, i: (b, i, g)),
        ],
        out_specs=pl.BlockSpec((None, tq, grp), lambda b, g, i: (b, i, g)),
        out_shape=jax.ShapeDtypeStruct((B, S, D_ATTN), F32),
        compiler_params=_params(("arbitrary", "arbitrary", "arbitrary")),
        name="slc_win",
    )(q, ksl, vsl, kwn, vwn, sel, expand, gates, ocmp)


def _out_proj_kernel(yc_ref, ya_ref, x_ref, w_ref, ga_ref, g1_ref, n2_ref, sh_ref, sc_ref, x1_ref, h2_ref):
    ya = _rms(ya_ref[...], ga_ref[...]).astype(BF16)
    mix = _dot(yc_ref[...], w_ref[0:D_CONV, :]) + _dot(ya, w_ref[D_CONV:D_CONV + D_ATTN, :])
    x1 = x_ref[...] + g1_ref[...] * mix
    x1_ref[...] = x1
    h2_ref[...] = (_rms(x1, n2_ref[...]) * (1.0 + sc_ref[...]) + sh_ref[...]).astype(BF16)


def _out_proj(yconv, yattn, x, w_out_b, gattn_g, mod, norm2_g):
    B, S, D = x.shape
    tm = min(TM_PROJ, S)
    tok = lambda width: pl.BlockSpec((None, tm, width), lambda b, i: (b, i, 0))
    vec = lambda width: pl.BlockSpec((1, width), lambda b, i: (0, 0))
    modspec = lambda k: pl.BlockSpec((None, None, 1, D), lambda b, i, k=k: (b, k, 0, 0))
    return pl.pallas_call(
        _out_proj_kernel,
        grid=(B, S // tm),
        in_specs=[tok(D_CONV), tok(D_ATTN), tok(D),
                  pl.BlockSpec((D_CONV + D_ATTN, D), lambda b, i: (0, 0), pipeline_mode=pl.Buffered(1)),
                  vec(D_ATTN), modspec(2), vec(D), modspec(3), modspec(4)],
        out_specs=[tok(D), tok(D)],
        out_shape=[jax.ShapeDtypeStruct((B, S, D), F32), jax.ShapeDtypeStruct((B, S, D), BF16)],
        compiler_params=_params(("arbitrary", "arbitrary")),
        name="out_proj",
    )(yconv, yattn, x, w_out_b, gattn_g, mod, norm2_g, mod, mod)


def _ffn_kernel(h_ref, w1_ref, w2_ref, x1_ref, g2_ref, nf_ref, o_ref, acc_ref, *, final_norm):
    f = pl.program_id(2)

    @pl.when(f == 0)
    def _():
        acc_ref[...] = jnp.zeros(acc_ref.shape, F32)

    a = jnp.maximum(_dot(h_ref[...], w1_ref[...]), 0.0)
    acc_ref[...] += _dot((a * a).astype(BF16), w2_ref[...])

    @pl.when(f == pl.num_programs(2) - 1)
    def _():
        x2 = x1_ref[...] + g2_ref[...] * acc_ref[...]
        o_ref[...] = _rms(x2, nf_ref[...]) if final_norm else x2


def _ffn(h2, w1_b, w2_b, x1, mod, normf_g, final_norm):
    B, S, D = x1.shape
    d_ff = w1_b.shape[1]
    tm = min(TM_FFN, S)
    tf = min(TF_FFN, d_ff)
    tok = pl.BlockSpec((None, tm, D), lambda b, i, f: (b, i, 0))
    return pl.pallas_call(
        functools.partial(_ffn_kernel, final_norm=final_norm),
        grid=(B, S // tm, d_ff // tf),
        in_specs=[tok,
                  pl.BlockSpec((D, tf), lambda b, i, f: (0, f)),
                  pl.BlockSpec((tf, D), lambda b, i, f: (f, 0)),
                  tok,
                  pl.BlockSpec((None, None, 1, D), lambda b, i, f: (b, 5, 0, 0)),
                  pl.BlockSpec((1, D), lambda b, i, f: (0, 0))],
        out_specs=tok,
        out_shape=jax.ShapeDtypeStruct((B, S, D), F32),
        scratch_shapes=[pltpu.VMEM((tm, D), F32)],
        compiler_params=_params(("arbitrary", "arbitrary", "arbitrary")),
        name="ffn",
    )(h2, w1_b, w2_b, x1, mod, normf_g)


def _rope_tables(seq):
    inv = ROPE_THETA ** (-jnp.arange(0, HEAD_DIM, 2, dtype=F32) / HEAD_DIM)
    ang = jnp.arange(seq, dtype=F32)[:, None] * inv[None, :]
    cos, sin = jnp.cos(ang), jnp.sin(ang)
    return jnp.concatenate([cos, cos], axis=-1), jnp.concatenate([-sin, sin], axis=-1)


def _cmp_to_slc(n_blk):
    ci = np.arange(n_blk)[:, None]
    sj = np.arange(LANES)[None, :]
    m = (ci * CMP_STRIDE <= sj * SLC_LEN + SLC_LEN - 1) & (ci * CMP_STRIDE + CMP_LEN - 1 >= sj * SLC_LEN)
    return jnp.asarray(m, dtype=BF16)


def _block_expand(seq):
    m = np.arange(LANES)[:, None] == (np.arange(seq)[None, :] // SLC_LEN)
    return jnp.asarray(m, dtype=BF16)


def kernel(x, c, w_ada, b_ada, norm1_g, w_in, conv_w, conv_b, cmp_pe_k, cmp_pe_v, cmp_w1_k, cmp_w2_k,
           cmp_w1_v, cmp_w2_v, gnorm_conv_g, gnorm_attn_g, w_out, norm2_g, w_ff1, w_ff2, normf_g):
    B, S, D = x.shape
    depth = w_ada.shape[0]
    assert S % TQ == 0 and S // SLC_LEN <= LANES and S >= WINDOW + TQ
    assert w_in.shape[2] == D_IN
    cos_f, sin_f = _rope_tables(S)
    c2s = _cmp_to_slc(S // CMP_STRIDE)
    expand = _block_expand(S)
    for l in range(depth):
        mod = _adaln(c, w_ada[l], b_ada[l]).reshape(B, 6, 1, D)
        w_in_b = jnp.pad(w_in[l].astype(BF16), ((0, 0), (0, D_IN_PAD - D_IN)))
        (yconv, q, kc, vc, ksl, vsl, kwn, vwn, gates) = _in_proj(
            x, mod, norm1_g[l][None], w_in_b, conv_w[l], conv_b[l][None], gnorm_conv_g[l][None], cos_f, sin_f)
        k_cmp = _compress(kc, cmp_pe_k[l], cmp_w1_k[l].astype(BF16), cmp_w2_k[l].astype(BF16))
        v_cmp = _compress(vc, cmp_pe_v[l], cmp_w1_v[l].astype(BF16), cmp_w2_v[l].astype(BF16))
        ocmp, sel = _cmp_select(q, k_cmp, v_cmp, c2s, gates)
        yattn = _slc_win(q, ksl, vsl, kwn, vwn, sel, expand, gates, ocmp)
        x1, h2 = _out_proj(yconv, yattn, x, w_out[l].astype(BF16), gnorm_attn_g[l][None], mod, norm2_g[l][None])
        x = _ffn(h2, w_ff1[l].astype(BF16), w_ff2[l].astype(BF16), x1, mod, normf_g[None],
                 final_norm=(l == depth - 1))
    return x
```

```python
import functools

import numpy as np
import jax
import jax.numpy as jnp
from jax import lax
from jax.experimental import pallas as pl
from jax.experimental.pallas import tpu as pltpu

F32 = jnp.float32
BF16 = jnp.bfloat16

HEAD_DIM = 128
N_HEADS = 8
N_KV_HEADS = 2
GQA_GROUP = N_HEADS // N_KV_HEADS
D_CONV = 1024
D_ATTN = N_HEADS * HEAD_DIM
D_KV = N_KV_HEADS * HEAD_DIM
N_BRANCH = 3
CONV_WIDTH = 3
CMP_LEN = 32
CMP_STRIDE = 16
CMP_HIDDEN = 256
SLC_LEN = 64
N_SLC = 16
WINDOW = 512
ROPE_THETA = 10000.0
EPS = 1e-6
FORCE_BONUS = 1e4

LANES = 128
SUBLANES = 8
VMEM_LIMIT_BYTES = 56 * 1024 * 1024

NEG = float(np.finfo(np.float32).min)
MASK_BIAS = -(2.0 ** 126)
LOG2E = float(np.log2(np.e))

COL_UB, COL_UC, COL_UH = 0, D_CONV, 2 * D_CONV
COL_Q = 3 * D_CONV
COL_KV = COL_Q + D_ATTN
COL_GATE = COL_KV + 2 * N_BRANCH * D_KV
D_IN = COL_GATE + N_BRANCH * N_HEADS
D_IN_PAD = COL_GATE + LANES

TM_PROJ = 512
TQ = 128
TQ_CMP = 256
TK_SLC = 1024
TM_FFN = 512
TF_FFN = 1024
TN_ADA = 1024


def _params(sem):
    return pltpu.CompilerParams(dimension_semantics=sem, vmem_limit_bytes=VMEM_LIMIT_BYTES)


def _dot(a, b):
    return jnp.dot(a, b, preferred_element_type=F32)


def _dot_nt(a, b):
    return lax.dot_general(a, b, (((1,), (1,)), ((), ())), preferred_element_type=F32)


def _rms(x, g):
    return x * lax.rsqrt(jnp.mean(x * x, axis=-1, keepdims=True) + EPS) * g


def _adaln_kernel(c_ref, w_ref, b_ref, o_ref):
    c = c_ref[...]
    s = c * jax.nn.sigmoid(c)
    o_ref[...] = _dot(s.astype(BF16), w_ref[...].astype(BF16)) + b_ref[...]


def _adaln(c, w_ada, b_ada):
    B, D = c.shape
    n_out = w_ada.shape[1]
    rows = -(-B // SUBLANES) * SUBLANES
    c_pad = jnp.zeros((rows, D), F32).at[:B].set(c)
    out = pl.pallas_call(
        _adaln_kernel,
        grid=(n_out // TN_ADA,),
        in_specs=[
            pl.BlockSpec((rows, D), lambda j: (0, 0)),
            pl.BlockSpec((D, TN_ADA), lambda j: (0, j)),
            pl.BlockSpec((1, TN_ADA), lambda j: (0, j)),
        ],
        out_specs=pl.BlockSpec((rows, TN_ADA), lambda j: (0, j)),
        out_shape=jax.ShapeDtypeStruct((rows, n_out), F32),
        compiler_params=_params(("arbitrary",)),
        name="adaln",
    )(c_pad, w_ada, b_ada.reshape(1, n_out))
    return out[:B]


def _rope(u, cos_f, sin_f):
    return u * cos_f + pltpu.roll(u, HEAD_DIM // 2, 1) * sin_f


def _in_proj_kernel(x_ref, sh_ref, sc_ref, g_ref, w_ref, cw_ref, cb_ref, gc_ref, cos_ref, sin_ref,
                    yconv_ref, q_ref, kc_ref, vc_ref, ksl_ref, vsl_ref, kwn_ref, vwn_ref, gate_ref,
                    vbuf_ref, *, tm, q_scale):
    i = pl.program_id(1)
    x = x_ref[...]
    h = _rms(x, g_ref[...]) * (1.0 + sc_ref[...]) + sh_ref[...]
    hb = h.astype(BF16)

    ub = _dot(hb, w_ref[:, COL_UB:COL_UB + D_CONV])
    uc = _dot(hb, w_ref[:, COL_UC:COL_UC + D_CONV])
    uh = _dot(hb, w_ref[:, COL_UH:COL_UH + D_CONV])
    v = uc * uh

    @pl.when(i == 0)
    def _():
        vbuf_ref[0:SUBLANES, :] = jnp.zeros((SUBLANES, D_CONV), F32)

    vbuf_ref[SUBLANES:SUBLANES + tm, :] = v
    v1 = vbuf_ref[SUBLANES - 1:SUBLANES - 1 + tm, :]
    v2 = vbuf_ref[SUBLANES - 2:SUBLANES - 2 + tm, :]
    z = cb_ref[...] + cw_ref[0:1, :] * v2 + cw_ref[1:2, :] * v1 + cw_ref[2:3, :] * v
    vbuf_ref[0:SUBLANES, :] = vbuf_ref[tm:tm + SUBLANES, :]
    yconv_ref[...] = _rms(ub * z, gc_ref[...]).astype(BF16)

    cos_f = cos_ref[...]
    sin_f = sin_ref[...]

    uq = _dot(hb, w_ref[:, COL_Q:COL_Q + D_ATTN])
    for hd in range(N_HEADS):
        sl = slice(hd * HEAD_DIM, (hd + 1) * HEAD_DIM)
        q_ref[:, sl] = (_rope(uq[:, sl], cos_f, sin_f) * q_scale).astype(BF16)

    ukv = _dot(hb, w_ref[:, COL_KV:COL_GATE])
    outs = (kc_ref, vc_ref, ksl_ref, vsl_ref, kwn_ref, vwn_ref)
    for n, o_ref in enumerate(outs):
        for g in range(N_KV_HEADS):
            src = slice(n * D_KV + g * HEAD_DIM, n * D_KV + (g + 1) * HEAD_DIM)
            dst = slice(g * HEAD_DIM, (g + 1) * HEAD_DIM)
            t = ukv[:, src]
            if n % 2 == 0:
                t = _rope(t, cos_f, sin_f)
            o_ref[:, dst] = t.astype(o_ref.dtype)

    ug = _dot(hb, w_ref[:, COL_GATE:D_IN_PAD])
    gate_ref[...] = jax.nn.sigmoid(ug)


def _in_proj(x, mod, norm1_g, w_in_b, conv_w, conv_b, gconv_g, cos_f, sin_f):
    B, S, D = x.shape
    tm = min(TM_PROJ, S)
    tok = lambda width: pl.BlockSpec((None, tm, width), lambda b, i: (b, i, 0))
    vec = lambda width: pl.BlockSpec((1, width), lambda b, i: (0, 0))
    modspec = lambda k: pl.BlockSpec((None, None, 1, D), lambda b, i, k=k: (b, k, 0, 0))
    kernel = functools.partial(_in_proj_kernel, tm=tm, q_scale=LOG2E * HEAD_DIM ** -0.5)
    sd = jax.ShapeDtypeStruct
    return pl.pallas_call(
        kernel,
        grid=(B, S // tm),
        in_specs=[
            tok(D), modspec(0), modspec(1), vec(D),
            pl.BlockSpec((D, D_IN_PAD), lambda b, i: (0, 0), pipeline_mode=pl.Buffered(1)),
            pl.BlockSpec((CONV_WIDTH, D_CONV), lambda b, i: (0, 0)), vec(D_CONV), vec(D_CONV),
            pl.BlockSpec((tm, HEAD_DIM), lambda b, i: (i, 0)),
            pl.BlockSpec((tm, HEAD_DIM), lambda b, i: (i, 0)),
        ],
        out_specs=[tok(D_CONV), tok(D_ATTN), tok(D_KV), tok(D_KV), tok(D_KV), tok(D_KV), tok(D_KV),
                   tok(D_KV), tok(LANES)],
        out_shape=[sd((B, S, D_CONV), BF16), sd((B, S, D_ATTN), BF16),
                   sd((B, S, D_KV), F32), sd((B, S, D_KV), F32),
                   sd((B, S, D_KV), BF16), sd((B, S, D_KV), BF16),
                   sd((B, S, D_KV), BF16), sd((B, S, D_KV), BF16),
                   sd((B, S, LANES), F32)],
        scratch_shapes=[pltpu.VMEM((tm + 2 * SUBLANES, D_CONV), F32)],
        compiler_params=_params(("arbitrary", "arbitrary")),
        name="in_proj",
    )(x, mod, mod, norm1_g, w_in_b, conv_w, conv_b, gconv_g, cos_f, sin_f)


def _compress_kernel(kv_ref, pe_ref, w1_ref, w2_ref, o_ref, buf_ref, *, seq, n_blk):
    buf_ref[0:seq, :] = kv_ref[...]
    buf_ref[seq:seq + CMP_STRIDE, :] = jnp.zeros((CMP_STRIDE, HEAD_DIM), F32)
    acc = jnp.zeros((n_blk, CMP_HIDDEN), F32)
    for l in range(CMP_LEN):
        rows = buf_ref[pl.ds(l, n_blk, stride=CMP_STRIDE), :] + pe_ref[l:l + 1, :]
        acc = acc + _dot(rows.astype(BF16), w1_ref[l * HEAD_DIM:(l + 1) * HEAD_DIM, :])
    hid = jax.nn.gelu(acc)
    o_ref[...] = _dot(hid.astype(BF16), w2_ref[...]).astype(BF16)


def _compress(kv, pe, w1_b, w2_b):
    B, S, _ = kv.shape
    n_blk = S // CMP_STRIDE
    kernel = functools.partial(_compress_kernel, seq=S, n_blk=n_blk)
    return pl.pallas_call(
        kernel,
        grid=(B, N_KV_HEADS),
        in_specs=[
            pl.BlockSpec((None, S, HEAD_DIM), lambda b, g: (b, 0, g)),
            pl.BlockSpec((CMP_LEN, HEAD_DIM), lambda b, g: (0, 0)),
            pl.BlockSpec((CMP_LEN * HEAD_DIM, CMP_HIDDEN), lambda b, g: (0, 0)),
            pl.BlockSpec((CMP_HIDDEN, HEAD_DIM), lambda b, g: (0, 0)),
        ],
        out_specs=pl.BlockSpec((None, None, n_blk, HEAD_DIM), lambda b, g: (b, g, 0, 0)),
        out_shape=jax.ShapeDtypeStruct((B, N_KV_HEADS, n_blk, HEAD_DIM), BF16),
        scratch_shapes=[pltpu.VMEM((S + CMP_STRIDE, HEAD_DIM), F32)],
        compiler_params=_params(("arbitrary", "arbitrary")),
        name="compress",
    )(kv, pe, w1_b, w2_b)


def _split3_dot(p, m01):
    hi = p.astype(BF16)
    r1 = p - hi.astype(F32)
    mid = r1.astype(BF16)
    lo = (r1 - mid.astype(F32)).astype(BF16)
    return _dot(hi, m01) + _dot(mid, m01) + _dot(lo, m01)


def _stack_heads(q_ref):
    return jnp.concatenate([q_ref[:, r * HEAD_DIM:(r + 1) * HEAD_DIM] for r in range(GQA_GROUP)], axis=0)


def _gate_col(gates, g, r, branch):
    lo = r * N_BRANCH + branch
    hi = (GQA_GROUP + r) * N_BRANCH + branch
    return jnp.where(g == 0, gates[:, lo:lo + 1], gates[:, hi:hi + 1])


def _cmp_select_kernel(q_ref, kc_ref, vc_ref, c2s_ref, tri_ref, gate_ref, ocmp_ref, sel_ref, *, tq, n_blk):
    g = pl.program_id(1)
    qs = pl.program_id(2) * tq
    t_row = qs + lax.broadcasted_iota(jnp.int32, (tq, n_blk), 0)
    cmp_end = lax.broadcasted_iota(jnp.int32, (tq, n_blk), 1) * CMP_STRIDE + (CMP_LEN - 1)
    mask = cmp_end <= t_row
    kc = kc_ref[...]
    vc = vc_ref[...]
    gates = gate_ref[...]
    psum = jnp.zeros((tq, n_blk), F32)
    for r in range(GQA_GROUP):
        cs = slice(r * HEAD_DIM, (r + 1) * HEAD_DIM)
        s = jnp.where(mask, _dot_nt(q_ref[:, cs], kc), NEG)
        m = jnp.max(s, axis=-1, keepdims=True)
        p = jnp.where(mask, jnp.exp2(s - m), 0.0)
        l = jnp.sum(p, axis=-1, keepdims=True)
        p = p / jnp.where(l > 0.0, l, 1.0)
        ocmp_ref[:, cs] = _gate_col(gates, g, r, 0) * _dot(p.astype(BF16), vc)
        psum = psum + p

    imp = _split3_dot(psum, c2s_ref[...])
    t = qs + lax.broadcasted_iota(jnp.int32, (tq, LANES), 0)
    blk = lax.broadcasted_iota(jnp.int32, (tq, LANES), 1)
    cur = t // SLC_LEN
    valid = blk * SLC_LEN <= t
    forced = (blk == 0) | (blk == cur) | (blk == cur - 1)
    score = jnp.where(valid, imp + jnp.where(forced, FORCE_BONUS, 0.0), -1.0)

    x = score
    covered = jnp.zeros((tq, 1), F32)
    n_before = jnp.zeros((tq, 1), F32)
    theta = jnp.full((tq, 1), -1.0, F32)
    for _ in range(N_SLC):
        mx = jnp.max(x, axis=-1, keepdims=True)
        eq = x == mx
        cnt = jnp.sum(jnp.where(eq, 1.0, 0.0), axis=-1, keepdims=True)
        crossing = (covered < float(N_SLC)) & (covered + cnt >= float(N_SLC))
        theta = jnp.where(crossing, mx, theta)
        n_before = jnp.where(crossing, covered, n_before)
        covered = covered + cnt
        x = jnp.where(eq, -jnp.inf, x)
    tie = score == theta
    tie_rank = _dot(jnp.where(tie, 1.0, 0.0).astype(BF16), tri_ref[...])
    chosen = (score > theta) | (tie & (tie_rank <= float(N_SLC) - n_before))
    sel_ref[...] = jnp.where(valid & chosen, 1.0, 0.0).astype(BF16)


def _cmp_select(q, k_cmp, v_cmp, c2s, tri, gates):
    B, S, _ = q.shape
    n_blk = k_cmp.shape[2]
    tq = min(TQ_CMP, S)
    kernel = functools.partial(_cmp_select_kernel, tq=tq, n_blk=n_blk)
    grp = GQA_GROUP * HEAD_DIM
    return pl.pallas_call(
        kernel,
        grid=(B, N_KV_HEADS, S // tq),
        in_specs=[
            pl.BlockSpec((None, tq, grp), lambda b, g, i: (b, i, g)),
            pl.BlockSpec((None, None, n_blk, HEAD_DIM), lambda b, g, i: (b, g, 0, 0)),
            pl.BlockSpec((None, None, n_blk, HEAD_DIM), lambda b, g, i: (b, g, 0, 0)),
            pl.BlockSpec((n_blk, LANES), lambda b, g, i: (0, 0)),
            pl.BlockSpec((LANES, LANES), lambda b, g, i: (0, 0)),
            pl.BlockSpec((None, tq, LANES), lambda b, g, i: (b, i, 0)),
        ],
        out_specs=[
            pl.BlockSpec((None, tq, grp), lambda b, g, i: (b, i, g)),
            pl.BlockSpec((None, None, tq, LANES), lambda b, g, i: (b, g, i, 0)),
        ],
        out_shape=[jax.ShapeDtypeStruct((B, S, D_ATTN), F32),
                   jax.ShapeDtypeStruct((B, N_KV_HEADS, S, LANES), BF16)],
        compiler_params=_params(("arbitrary", "arbitrary", "arbitrary")),
        name="cmp_select",
    )(q, k_cmp, v_cmp, c2s, tri, gates)


def _slc_win_kernel(q_ref, ksl_ref, vsl_ref, kwn_ref, vwn_ref, sel_ref, onehot_ref, wbias_ref, gate_ref,
                    ocmp_ref, y_ref, s_ref, *, tq, tk):
    g = pl.program_id(1)
    qs = pl.program_id(2) * tq
    rows = GQA_GROUP * tq
    qst = _stack_heads(q_ref)
    ones_k = jnp.ones((tk, HEAD_DIM), BF16)
    ones_q = jnp.ones((tq, HEAD_DIM), BF16)

    row_id = lax.broadcasted_iota(jnp.int32, (rows, tq), 0) & (tq - 1)
    eye = jnp.where(row_id == lax.broadcasted_iota(jnp.int32, (rows, tq), 1), 1.0, 0.0).astype(BF16)
    q_band = jnp.concatenate([qst, eye], axis=1)
    n_chunk = WINDOW // tq + 1
    k_parts, v_parts = [], []
    for c in range(n_chunk):
        start = qs - WINDOW + c * tq
        src = pl.multiple_of(jnp.maximum(start, 0), tq)
        bias_c = wbias_ref[c * tq:(c + 1) * tq, :]
        bias_c = jnp.where(start >= 0, bias_c, jnp.full(bias_c.shape, MASK_BIAS, BF16))
        k_parts.append(jnp.concatenate([kwn_ref[pl.ds(src, tq), :], bias_c], axis=1))
        v_parts.append(jnp.concatenate([vwn_ref[pl.ds(src, tq), :], ones_q], axis=1))
    own = pl.ds(pl.multiple_of(qs, tq), tq)
    causal_bias = wbias_ref[(n_chunk - 1) * tq:n_chunk * tq, :]
    k_parts.append(jnp.concatenate([ksl_ref[own, :], causal_bias], axis=1))
    s_band = _dot_nt(q_band, jnp.concatenate(k_parts, axis=0))
    s_w = s_band[:, 0:WINDOW + tq]
    p_w = jnp.exp2(s_w - jnp.max(s_w, axis=-1, keepdims=True))
    acc_w = _dot(p_w.astype(BF16), jnp.concatenate(v_parts, axis=0))
    l_w = acc_w[:, HEAD_DIM:HEAD_DIM + 1]
    o_win = acc_w[:, 0:HEAD_DIM] / jnp.where(l_w > 0.0, l_w, 1.0)
    s_own = s_band[:, WINDOW + tq:WINDOW + 2 * tq]
    m_own = jnp.max(s_own, axis=-1, keepdims=True)
    acc_own = _dot(jnp.exp2(s_own - m_own).astype(BF16), jnp.concatenate([vsl_ref[own, :], ones_q], axis=1))

    blk = lax.broadcasted_iota(jnp.int32, (tq, LANES), 1)
    sel_bias = (sel_ref[...].astype(F32) - 1.0) * (-MASK_BIAS)
    sel_bias = jnp.where(blk < qs // SLC_LEN, sel_bias, MASK_BIAS).astype(BF16)
    q_slc = jnp.concatenate([qst, jnp.concatenate([sel_bias] * GQA_GROUP, axis=0)], axis=1)

    def scores(kt):
        k0 = pl.multiple_of(kt * tk, tk)
        k_aug = jnp.concatenate([ksl_ref[pl.ds(k0, tk), :], onehot_ref[pl.ds(k0, tk), :]], axis=1)
        return _dot_nt(q_slc, k_aug)

    def update(kt, s, carry):
        m, acc = carry
        k0 = pl.multiple_of(kt * tk, tk)
        m_new = jnp.maximum(m, jnp.max(s, axis=-1, keepdims=True))
        p = jnp.exp2(s - m_new)
        alpha = jnp.exp2(m - m_new)
        v_aug = jnp.concatenate([vsl_ref[pl.ds(k0, tk), :], ones_k], axis=1)
        return m_new, alpha * acc + _dot(p.astype(BF16), v_aug)

    n_kt = jnp.maximum((qs + tk - 1) // tk, 1)
    s_ref[0] = scores(0)

    def body(kt, carry):
        slot = kt & 1
        s_cur = s_ref[slot]
        s_ref[1 - slot] = scores(kt + 1)
        return update(kt, s_cur, carry)

    carry = lax.fori_loop(0, n_kt - 1, body, (m_own, acc_own))
    last = n_kt - 1
    _, acc_s = update(last, s_ref[last & 1], carry)
    l_s = acc_s[:, HEAD_DIM:HEAD_DIM + 1]
    o_slc = acc_s[:, 0:HEAD_DIM] / jnp.where(l_s > 0.0, l_s, 1.0)

    gates = gate_ref[...]
    for r in range(GQA_GROUP):
        rs = slice(r * tq, (r + 1) * tq)
        cs = slice(r * HEAD_DIM, (r + 1) * HEAD_DIM)
        y_ref[:, cs] = (ocmp_ref[:, cs] + _gate_col(gates, g, r, 1) * o_slc[rs, :]
                        + _gate_col(gates, g, r, 2) * o_win[rs, :])


def _slc_win(q, ksl, vsl, kwn, vwn, sel, onehot, wbias, gates, ocmp):
    B, S, _ = q.shape
    tq = min(TQ, S)
    tk = min(TK_SLC, S)
    grp = GQA_GROUP * HEAD_DIM
    kernel = functools.partial(_slc_win_kernel, tq=tq, tk=tk)
    kv = pl.BlockSpec((None, S, HEAD_DIM), lambda b, g, i: (b, 0, g))
    return pl.pallas_call(
        kernel,
        grid=(B, N_KV_HEADS, S // tq),
        in_specs=[
            pl.BlockSpec((None, tq, grp), lambda b, g, i: (b, i, g)),
            kv, kv, kv, kv,
            pl.BlockSpec((None, None, tq, LANES), lambda b, g, i: (b, g, i, 0)),
            pl.BlockSpec((S, LANES), lambda b, g, i: (0, 0)),
            pl.BlockSpec((WINDOW + tq, tq), lambda b, g, i: (0, 0)),
            pl.BlockSpec((None, tq, LANES), lambda b, g, i: (b, i, 0)),
            pl.BlockSpec((None, tq, grp), lambda b, g, i: (b, i, g)),
        ],
        out_specs=pl.BlockSpec((None, tq, grp), lambda b, g, i: (b, i, g)),
        out_shape=jax.ShapeDtypeStruct((B, S, D_ATTN), F32),
        scratch_shapes=[pltpu.VMEM((2, GQA_GROUP * tq, tk), F32)],
        compiler_params=_params(("arbitrary", "arbitrary", "arbitrary")),
        name="slc_win",
    )(q, ksl, vsl, kwn, vwn, sel, onehot, wbias, gates, ocmp)


def _out_proj_kernel(yc_ref, ya_ref, x_ref, w_ref, ga_ref, g1_ref, n2_ref, sh_ref, sc_ref, x1_ref, h2_ref):
    ya = _rms(ya_ref[...], ga_ref[...]).astype(BF16)
    mix = _dot(yc_ref[...], w_ref[0:D_CONV, :]) + _dot(ya, w_ref[D_CONV:D_CONV + D_ATTN, :])
    x1 = x_ref[...] + g1_ref[...] * mix
    x1_ref[...] = x1
    h2_ref[...] = (_rms(x1, n2_ref[...]) * (1.0 + sc_ref[...]) + sh_ref[...]).astype(BF16)


def _out_proj(yconv, yattn, x, w_out_b, gattn_g, mod, norm2_g):
    B, S, D = x.shape
    tm = min(TM_PROJ, S)
    tok = lambda width: pl.BlockSpec((None, tm, width), lambda b, i: (b, i, 0))
    vec = lambda width: pl.BlockSpec((1, width), lambda b, i: (0, 0))
    modspec = lambda k: pl.BlockSpec((None, None, 1, D), lambda b, i, k=k: (b, k, 0, 0))
    return pl.pallas_call(
        _out_proj_kernel,
        grid=(B, S // tm),
        in_specs=[tok(D_CONV), tok(D_ATTN), tok(D),
                  pl.BlockSpec((D_CONV + D_ATTN, D), lambda b, i: (0, 0), pipeline_mode=pl.Buffered(1)),
                  vec(D_ATTN), modspec(2), vec(D), modspec(3), modspec(4)],
        out_specs=[tok(D), tok(D)],
        out_shape=[jax.ShapeDtypeStruct((B, S, D), F32), jax.ShapeDtypeStruct((B, S, D), BF16)],
        compiler_params=_params(("arbitrary", "arbitrary")),
        name="out_proj",
    )(yconv, yattn, x, w_out_b, gattn_g, mod, norm2_g, mod, mod)


def _ffn_kernel(h_ref, w1_ref, w2_ref, x1_ref, g2_ref, nf_ref, o_ref, acc_ref, *, final_norm):
    f = pl.program_id(2)

    @pl.when(f == 0)
    def _():
        acc_ref[...] = jnp.zeros(acc_ref.shape, F32)

    a = jnp.maximum(_dot(h_ref[...], w1_ref[...]), 0.0)
    acc_ref[...] += _dot((a * a).astype(BF16), w2_ref[...])

    @pl.when(f == pl.num_programs(2) - 1)
    def _():
        x2 = x1_ref[...] + g2_ref[...] * acc_ref[...]
        o_ref[...] = _rms(x2, nf_ref[...]) if final_norm else x2


def _ffn(h2, w1_b, w2_b, x1, mod, normf_g, final_norm):
    B, S, D = x1.shape
    d_ff = w1_b.shape[1]
    tm = min(TM_FFN, S)
    tf = min(TF_FFN, d_ff)
    tok = pl.BlockSpec((None, tm, D), lambda b, i, f: (b, i, 0))
    return pl.pallas_call(
        functools.partial(_ffn_kernel, final_norm=final_norm),
        grid=(B, S // tm, d_ff // tf),
        in_specs=[tok,
                  pl.BlockSpec((D, tf), lambda b, i, f: (0, f)),
                  pl.BlockSpec((tf, D), lambda b, i, f: (f, 0)),
                  tok,
                  pl.BlockSpec((None, None, 1, D), lambda b, i, f: (b, 5, 0, 0)),
                  pl.BlockSpec((1, D), lambda b, i, f: (0, 0))],
        out_specs=tok,
        out_shape=jax.ShapeDtypeStruct((B, S, D), F32),
        scratch_shapes=[pltpu.VMEM((tm, D), F32)],
        compiler_params=_params(("arbitrary", "arbitrary", "arbitrary")),
        name="ffn",
    )(h2, w1_b, w2_b, x1, mod, normf_g)


def _rope_tables(seq):
    inv = ROPE_THETA ** (-jnp.arange(0, HEAD_DIM, 2, dtype=F32) / HEAD_DIM)
    ang = jnp.arange(seq, dtype=F32)[:, None] * inv[None, :]
    cos, sin = jnp.cos(ang), jnp.sin(ang)
    return jnp.concatenate([cos, cos], axis=-1), jnp.concatenate([-sin, sin], axis=-1)


def _cmp_to_slc(n_blk):
    ci = np.arange(n_blk)[:, None]
    sj = np.arange(LANES)[None, :]
    m = (ci * CMP_STRIDE <= sj * SLC_LEN + SLC_LEN - 1) & (ci * CMP_STRIDE + CMP_LEN - 1 >= sj * SLC_LEN)
    return jnp.asarray(m, dtype=BF16)


def _block_onehot(seq):
    m = (np.arange(seq)[:, None] // SLC_LEN) == np.arange(LANES)[None, :]
    return jnp.asarray(m, dtype=BF16)


def _prefix_ones():
    return jnp.asarray(np.arange(LANES)[:, None] <= np.arange(LANES)[None, :], dtype=BF16)


def _window_bias(tq):
    key = np.arange(WINDOW + tq)[:, None]
    row = np.arange(tq)[None, :]
    ok = (key > row) & (key <= row + WINDOW)
    return jnp.asarray(np.where(ok, 0.0, MASK_BIAS), dtype=BF16)


def kernel(x, c, w_ada, b_ada, norm1_g, w_in, conv_w, conv_b, cmp_pe_k, cmp_pe_v, cmp_w1_k, cmp_w2_k,
           cmp_w1_v, cmp_w2_v, gnorm_conv_g, gnorm_attn_g, w_out, norm2_g, w_ff1, w_ff2, normf_g):
    B, S, D = x.shape
    depth = w_ada.shape[0]
    assert S % TQ == 0 and S // SLC_LEN <= LANES and S >= WINDOW + TQ
    assert w_in.shape[2] == D_IN
    cos_f, sin_f = _rope_tables(S)
    c2s = _cmp_to_slc(S // CMP_STRIDE)
    onehot = _block_onehot(S)
    tri = _prefix_ones()
    wbias = _window_bias(min(TQ, S))
    for l in range(depth):
        mod = _adaln(c, w_ada[l], b_ada[l]).reshape(B, 6, 1, D)
        w_in_b = jnp.pad(w_in[l].astype(BF16), ((0, 0), (0, D_IN_PAD - D_IN)))
        (yconv, q, kc, vc, ksl, vsl, kwn, vwn, gates) = _in_proj(
            x, mod, norm1_g[l][None], w_in_b, conv_w[l], conv_b[l][None], gnorm_conv_g[l][None], cos_f, sin_f)
        k_cmp = _compress(kc, cmp_pe_k[l], cmp_w1_k[l].astype(BF16), cmp_w2_k[l].astype(BF16))
        v_cmp = _compress(vc, cmp_pe_v[l], cmp_w1_v[l].astype(BF16), cmp_w2_v[l].astype(BF16))
        ocmp, sel = _cmp_select(q, k_cmp, v_cmp, c2s, tri, gates)
        yattn = _slc_win(q, ksl, vsl, kwn, vwn, sel, onehot, wbias, gates, ocmp)
        x1, h2 = _out_proj(yconv, yattn, x, w_out[l].astype(BF16), gnorm_attn_g[l][None], mod, norm2_g[l][None])
        x = _ffn(h2, w_ff1[l].astype(BF16), w_ff2[l].astype(BF16), x1, mod, normf_g[None],
                 final_norm=(l == depth - 1))
    return x
```

```python
import functools

import numpy as np
import jax
import jax.numpy as jnp
from jax import lax
from jax.experimental import pallas as pl
from jax.experimental.pallas import tpu as pltpu

F32 = jnp.float32
BF16 = jnp.bfloat16

HEAD_DIM = 128
N_HEADS = 8
N_KV_HEADS = 2
GQA_GROUP = N_HEADS // N_KV_HEADS
D_CONV = 1024
D_ATTN = N_HEADS * HEAD_DIM
D_KV = N_KV_HEADS * HEAD_DIM
N_BRANCH = 3
CONV_WIDTH = 3
CMP_LEN = 32
CMP_STRIDE = 16
CMP_HIDDEN = 256
SLC_LEN = 64
N_SLC = 16
WINDOW = 512
ROPE_THETA = 10000.0
EPS = 1e-6
FORCE_BONUS = 1e4

LANES = 128
SUBLANES = 8
VMEM_LIMIT_BYTES = 56 * 1024 * 1024

NEG = float(np.finfo(np.float32).min)
MASK_BIAS = -(2.0 ** 126)
LOG2E = float(np.log2(np.e))

COL_UB, COL_UC, COL_UH = 0, D_CONV, 2 * D_CONV
COL_Q = 3 * D_CONV
COL_KV = COL_Q + D_ATTN
COL_GATE = COL_KV + 2 * N_BRANCH * D_KV
D_IN = COL_GATE + N_BRANCH * N_HEADS
D_IN_PAD = COL_GATE + LANES

TM_PROJ = 512
TQ = 128
TQ_CMP = 256
TK_SLC = 1024
TM_FFN = 512
TF_FFN = 1024
TN_ADA = 1024


def _params(sem):
    return pltpu.CompilerParams(dimension_semantics=sem, vmem_limit_bytes=VMEM_LIMIT_BYTES)


def _dot(a, b):
    return jnp.dot(a, b, preferred_element_type=F32)


def _dot_nt(a, b):
    return lax.dot_general(a, b, (((1,), (1,)), ((), ())), preferred_element_type=F32)


def _rms(x, g):
    return x * lax.rsqrt(jnp.mean(x * x, axis=-1, keepdims=True) + EPS) * g


def _adaln_kernel(c_ref, w_ref, b_ref, o_ref):
    c = c_ref[...]
    s = c * jax.nn.sigmoid(c)
    o_ref[...] = _dot(s.astype(BF16), w_ref[...].astype(BF16)) + b_ref[...]


def _adaln(c, w_ada, b_ada):
    B, D = c.shape
    n_out = w_ada.shape[1]
    rows = -(-B // SUBLANES) * SUBLANES
    c_pad = jnp.zeros((rows, D), F32).at[:B].set(c)
    out = pl.pallas_call(
        _adaln_kernel,
        grid=(n_out // TN_ADA,),
        in_specs=[
            pl.BlockSpec((rows, D), lambda j: (0, 0)),
            pl.BlockSpec((D, TN_ADA), lambda j: (0, j)),
            pl.BlockSpec((1, TN_ADA), lambda j: (0, j)),
        ],
        out_specs=pl.BlockSpec((rows, TN_ADA), lambda j: (0, j)),
        out_shape=jax.ShapeDtypeStruct((rows, n_out), F32),
        compiler_params=_params(("arbitrary",)),
        name="adaln",
    )(c_pad, w_ada, b_ada.reshape(1, n_out))
    return out[:B]


def _rope(u, cos_f, sin_f):
    return u * cos_f + pltpu.roll(u, HEAD_DIM // 2, 1) * sin_f


def _in_proj_kernel(x_ref, sh_ref, sc_ref, g_ref, w_ref, cw_ref, cb_ref, gc_ref, cos_ref, sin_ref,
                    yconv_ref, q_ref, kc_ref, vc_ref, ksl_ref, vsl_ref, kwn_ref, vwn_ref, gate_ref,
                    vbuf_ref, *, tm, q_scale):
    @pl.when(pl.program_id(1) == 0)
    def _():
        vbuf_ref[tm:tm + SUBLANES, :] = jnp.zeros((SUBLANES, D_CONV), F32)

    x = x_ref[...]
    h = _rms(x, g_ref[...]) * (1.0 + sc_ref[...]) + sh_ref[...]
    hb = h.astype(BF16)

    ub = _dot(hb, w_ref[:, COL_UB:COL_UB + D_CONV])
    uc = _dot(hb, w_ref[:, COL_UC:COL_UC + D_CONV])
    uh = _dot(hb, w_ref[:, COL_UH:COL_UH + D_CONV])
    v = uc * uh
    vbuf_ref[0:SUBLANES, :] = vbuf_ref[tm:tm + SUBLANES, :]
    vbuf_ref[SUBLANES:SUBLANES + tm, :] = v
    v1 = vbuf_ref[SUBLANES - 1:SUBLANES - 1 + tm, :]
    v2 = vbuf_ref[SUBLANES - 2:SUBLANES - 2 + tm, :]
    z = cb_ref[...] + cw_ref[0:1, :] * v2 + cw_ref[1:2, :] * v1 + cw_ref[2:3, :] * v
    yconv_ref[...] = _rms(ub * z, gc_ref[...]).astype(BF16)

    cos_f = cos_ref[...]
    sin_f = sin_ref[...]

    uq = _dot(hb, w_ref[:, COL_Q:COL_Q + D_ATTN])
    for hd in range(N_HEADS):
        sl = slice(hd * HEAD_DIM, (hd + 1) * HEAD_DIM)
        q_ref[:, sl] = (_rope(uq[:, sl], cos_f, sin_f) * q_scale).astype(BF16)

    ukv = _dot(hb, w_ref[:, COL_KV:COL_GATE])
    outs = (kc_ref, vc_ref, ksl_ref, vsl_ref, kwn_ref, vwn_ref)
    for n, o_ref in enumerate(outs):
        for g in range(N_KV_HEADS):
            src = slice(n * D_KV + g * HEAD_DIM, n * D_KV + (g + 1) * HEAD_DIM)
            dst = slice(g * HEAD_DIM, (g + 1) * HEAD_DIM)
            t = ukv[:, src]
            if n % 2 == 0:
                t = _rope(t, cos_f, sin_f)
            o_ref[:, dst] = t.astype(o_ref.dtype)

    ug = _dot(hb, w_ref[:, COL_GATE:D_IN_PAD])
    gate_ref[...] = jax.nn.sigmoid(ug)


def _in_proj(x, mod, norm1_g, w_in_b, conv_w, conv_b, gconv_g, cos_f, sin_f):
    B, S, D = x.shape
    tm = min(TM_PROJ, S)
    tok = lambda width: pl.BlockSpec((None, tm, width), lambda b, i: (b, i, 0))
    vec = lambda width: pl.BlockSpec((1, width), lambda b, i: (0, 0))
    modspec = lambda k: pl.BlockSpec((None, None, 1, D), lambda b, i, k=k: (b, k, 0, 0))
    kernel = functools.partial(_in_proj_kernel, tm=tm, q_scale=LOG2E * HEAD_DIM ** -0.5)
    sd = jax.ShapeDtypeStruct
    return pl.pallas_call(
        kernel,
        grid=(B, S // tm),
        in_specs=[
            tok(D), modspec(0), modspec(1), vec(D),
            pl.BlockSpec((D, D_IN_PAD), lambda b, i: (0, 0), pipeline_mode=pl.Buffered(1)),
            pl.BlockSpec((CONV_WIDTH, D_CONV), lambda b, i: (0, 0)), vec(D_CONV), vec(D_CONV),
            pl.BlockSpec((tm, HEAD_DIM), lambda b, i: (i, 0)),
            pl.BlockSpec((tm, HEAD_DIM), lambda b, i: (i, 0)),
        ],
        out_specs=[tok(D_CONV), tok(D_ATTN), tok(D_KV), tok(D_KV), tok(D_KV), tok(D_KV), tok(D_KV),
                   tok(D_KV), tok(LANES)],
        out_shape=[sd((B, S, D_CONV), BF16), sd((B, S, D_ATTN), BF16),
                   sd((B, S, D_KV), F32), sd((B, S, D_KV), F32),
                   sd((B, S, D_KV), BF16), sd((B, S, D_KV), BF16),
                   sd((B, S, D_KV), BF16), sd((B, S, D_KV), BF16),
                   sd((B, S, LANES), F32)],
        scratch_shapes=[pltpu.VMEM((tm + 2 * SUBLANES, D_CONV), F32)],
        compiler_params=_params(("arbitrary", "arbitrary")),
        name="in_proj",
    )(x, mod, mod, norm1_g, w_in_b, conv_w, conv_b, gconv_g, cos_f, sin_f)


def _compress_kernel(kv_ref, pe_ref, w1_ref, w2_ref, o_ref, buf_ref, *, seq, n_blk):
    buf_ref[0:seq, :] = kv_ref[...]
    buf_ref[seq:seq + CMP_STRIDE, :] = jnp.zeros((CMP_STRIDE, HEAD_DIM), F32)
    acc = jnp.zeros((n_blk, CMP_HIDDEN), F32)
    for l in range(CMP_LEN):
        rows = buf_ref[pl.ds(l, n_blk, stride=CMP_STRIDE), :] + pe_ref[l:l + 1, :]
        acc = acc + _dot(rows.astype(BF16), w1_ref[l * HEAD_DIM:(l + 1) * HEAD_DIM, :])
    hid = jax.nn.gelu(acc)
    o_ref[...] = _dot(hid.astype(BF16), w2_ref[...]).astype(BF16)


def _compress(kv, pe, w1_b, w2_b):
    B, S, _ = kv.shape
    n_blk = S // CMP_STRIDE
    kernel = functools.partial(_compress_kernel, seq=S, n_blk=n_blk)
    return pl.pallas_call(
        kernel,
        grid=(B, N_KV_HEADS),
        in_specs=[
            pl.BlockSpec((None, S, HEAD_DIM), lambda b, g: (b, 0, g)),
            pl.BlockSpec((CMP_LEN, HEAD_DIM), lambda b, g: (0, 0)),
            pl.BlockSpec((CMP_LEN * HEAD_DIM, CMP_HIDDEN), lambda b, g: (0, 0)),
            pl.BlockSpec((CMP_HIDDEN, HEAD_DIM), lambda b, g: (0, 0)),
        ],
        out_specs=pl.BlockSpec((None, None, n_blk, HEAD_DIM), lambda b, g: (b, g, 0, 0)),
        out_shape=jax.ShapeDtypeStruct((B, N_KV_HEADS, n_blk, HEAD_DIM), BF16),
        scratch_shapes=[pltpu.VMEM((S + CMP_STRIDE, HEAD_DIM), F32)],
        compiler_params=_params(("arbitrary", "arbitrary")),
        name="compress",
    )(kv, pe, w1_b, w2_b)


def _split3_dot_nt(m01, p):
    hi = p.astype(BF16)
    r1 = p - hi.astype(F32)
    mid = r1.astype(BF16)
    lo = (r1 - mid.astype(F32)).astype(BF16)
    return _dot_nt(m01, hi) + _dot_nt(m01, mid) + _dot_nt(m01, lo)


def _stack_heads(q_ref):
    return jnp.concatenate([q_ref[:, r * HEAD_DIM:(r + 1) * HEAD_DIM] for r in range(GQA_GROUP)], axis=0)


def _gate_col(gates, g, r, branch):
    lo = r * N_BRANCH + branch
    hi = (GQA_GROUP + r) * N_BRANCH + branch
    return jnp.where(g == 0, gates[:, lo:lo + 1], gates[:, hi:hi + 1])


def _cmp_select_kernel(q_ref, kc_ref, vc_ref, s2c_ref, tri_ref, eye_ref, gate_ref, ocmp_ref, sel_ref, *,
                       tq, n_blk):
    g = pl.program_id(1)
    qs = pl.program_id(2) * tq
    t_row = qs + lax.broadcasted_iota(jnp.int32, (tq, n_blk), 0)
    cmp_end = lax.broadcasted_iota(jnp.int32, (tq, n_blk), 1) * CMP_STRIDE + (CMP_LEN - 1)
    mask = cmp_end <= t_row
    kc = kc_ref[...]
    vc = vc_ref[...]
    gates = gate_ref[...]
    psum = jnp.zeros((tq, n_blk), F32)
    for r in range(GQA_GROUP):
        cs = slice(r * HEAD_DIM, (r + 1) * HEAD_DIM)
        s = jnp.where(mask, _dot_nt(q_ref[:, cs], kc), NEG)
        m = jnp.max(s, axis=-1, keepdims=True)
        p = jnp.where(mask, jnp.exp2(s - m), 0.0)
        l = jnp.sum(p, axis=-1, keepdims=True)
        p = p / jnp.where(l > 0.0, l, 1.0)
        ocmp_ref[:, cs] = _gate_col(gates, g, r, 0) * _dot(p.astype(BF16), vc)
        psum = psum + p

    imp = _split3_dot_nt(s2c_ref[...], psum)
    t = qs + lax.broadcasted_iota(jnp.int32, (LANES, tq), 1)
    blk = lax.broadcasted_iota(jnp.int32, (LANES, tq), 0)
    cur = t // SLC_LEN
    valid = blk * SLC_LEN <= t
    forced = (blk == 0) | (blk == cur) | (blk == cur - 1)
    score = jnp.where(valid, imp + jnp.where(forced, FORCE_BONUS, 0.0), -1.0)

    x = score
    covered = jnp.zeros((1, tq), F32)
    n_before = jnp.zeros((1, tq), F32)
    theta = jnp.full((1, tq), -1.0, F32)
    for _ in range(N_SLC):
        mx = jnp.max(x, axis=0, keepdims=True)
        eq = x == mx
        cnt = jnp.sum(jnp.where(eq, 1.0, 0.0), axis=0, keepdims=True)
        crossing = (covered < float(N_SLC)) & (covered + cnt >= float(N_SLC))
        theta = jnp.where(crossing, mx, theta)
        n_before = jnp.where(crossing, covered, n_before)
        covered = covered + cnt
        x = jnp.where(eq, -jnp.inf, x)
    tie = score == theta
    tie_rank = _dot(tri_ref[...], jnp.where(tie, 1.0, 0.0).astype(BF16))
    chosen = (score > theta) | (tie & (tie_rank <= float(N_SLC) - n_before))
    sel_t = jnp.where(valid & chosen, 1.0, 0.0).astype(BF16)
    sel_ref[...] = _dot_nt(eye_ref[...], sel_t).astype(BF16)


def _cmp_select(q, k_cmp, v_cmp, s2c, tri, eye, gates):
    B, S, _ = q.shape
    n_blk = k_cmp.shape[2]
    tq = min(TQ_CMP, S)
    kernel = functools.partial(_cmp_select_kernel, tq=tq, n_blk=n_blk)
    grp = GQA_GROUP * HEAD_DIM
    return pl.pallas_call(
        kernel,
        grid=(B, N_KV_HEADS, S // tq),
        in_specs=[
            pl.BlockSpec((None, tq, grp), lambda b, g, i: (b, i, g)),
            pl.BlockSpec((None, None, n_blk, HEAD_DIM), lambda b, g, i: (b, g, 0, 0)),
            pl.BlockSpec((None, None, n_blk, HEAD_DIM), lambda b, g, i: (b, g, 0, 0)),
            pl.BlockSpec((LANES, n_blk), lambda b, g, i: (0, 0)),
            pl.BlockSpec((LANES, LANES), lambda b, g, i: (0, 0)),
            pl.BlockSpec((tq, tq), lambda b, g, i: (0, 0)),
            pl.BlockSpec((None, tq, LANES), lambda b, g, i: (b, i, 0)),
        ],
        out_specs=[
            pl.BlockSpec((None, tq, grp), lambda b, g, i: (b, i, g)),
            pl.BlockSpec((None, None, tq, LANES), lambda b, g, i: (b, g, i, 0)),
        ],
        out_shape=[jax.ShapeDtypeStruct((B, S, D_ATTN), F32),
                   jax.ShapeDtypeStruct((B, N_KV_HEADS, S, LANES), BF16)],
        compiler_params=_params(("arbitrary", "arbitrary", "arbitrary")),
        name="cmp_select",
    )(q, k_cmp, v_cmp, s2c, tri, eye, gates)


def _slc_win_kernel(q_ref, ksl_ref, vsl_ref, kwn_ref, vwn_ref, sel_ref, onehot_ref, wbias_ref, gate_ref,
                    ocmp_ref, y_ref, s_ref, smax_ref, *, tq, tk):
    g = pl.program_id(1)
    qs = pl.program_id(2) * tq
    rows = GQA_GROUP * tq
    qst = _stack_heads(q_ref)
    ones_k = jnp.ones((tk, HEAD_DIM), BF16)
    ones_q = jnp.ones((tq, HEAD_DIM), BF16)

    row_id = lax.broadcasted_iota(jnp.int32, (rows, tq), 0) & (tq - 1)
    eye = jnp.where(row_id == lax.broadcasted_iota(jnp.int32, (rows, tq), 1), 1.0, 0.0).astype(BF16)
    q_band = jnp.concatenate([qst, eye], axis=1)
    n_chunk = WINDOW // tq + 1
    k_parts, v_parts = [], []
    for c in range(n_chunk):
        start = qs - WINDOW + c * tq
        src = pl.multiple_of(jnp.maximum(start, 0), tq)
        bias_c = wbias_ref[c * tq:(c + 1) * tq, :]
        bias_c = jnp.where(start >= 0, bias_c, jnp.full(bias_c.shape, MASK_BIAS, BF16))
        k_parts.append(jnp.concatenate([kwn_ref[pl.ds(src, tq), :], bias_c], axis=1))
        v_parts.append(jnp.concatenate([vwn_ref[pl.ds(src, tq), :], ones_q], axis=1))
    own = pl.ds(pl.multiple_of(qs, tq), tq)
    causal_bias = wbias_ref[(n_chunk - 1) * tq:n_chunk * tq, :]
    k_parts.append(jnp.concatenate([ksl_ref[own, :], causal_bias], axis=1))
    s_band = _dot_nt(q_band, jnp.concatenate(k_parts, axis=0))
    s_w = s_band[:, 0:WINDOW + tq]
    p_w = jnp.exp2(s_w - jnp.max(s_w, axis=-1, keepdims=True))
    acc_w = _dot(p_w.astype(BF16), jnp.concatenate(v_parts, axis=0))
    l_w = acc_w[:, HEAD_DIM:HEAD_DIM + 1]
    o_win = acc_w[:, 0:HEAD_DIM] / jnp.where(l_w > 0.0, l_w, 1.0)
    s_own = s_band[:, WINDOW + tq:WINDOW + 2 * tq]
    m_own = jnp.max(s_own, axis=-1, keepdims=True)
    acc_own = _dot(jnp.exp2(s_own - m_own).astype(BF16), jnp.concatenate([vsl_ref[own, :], ones_q], axis=1))

    blk = lax.broadcasted_iota(jnp.int32, (tq, LANES), 1)
    sel_bias = (sel_ref[...].astype(F32) - 1.0) * (-MASK_BIAS)
    sel_bias = jnp.where(blk < qs // SLC_LEN, sel_bias, MASK_BIAS).astype(BF16)
    q_slc = jnp.concatenate([qst, jnp.concatenate([sel_bias] * GQA_GROUP, axis=0)], axis=1)

    def put_scores(kt, slot):
        k0 = pl.multiple_of(kt * tk, tk)
        k_aug = jnp.concatenate([ksl_ref[pl.ds(k0, tk), :], onehot_ref[pl.ds(k0, tk), :]], axis=1)
        s = _dot_nt(q_slc, k_aug)
        s_ref[slot] = s
        smax_ref[slot] = jnp.broadcast_to(jnp.max(s, axis=-1, keepdims=True), (rows, LANES))

    def update(kt, s, s_max, carry):
        m, acc = carry
        k0 = pl.multiple_of(kt * tk, tk)
        m_new = jnp.maximum(m, s_max)
        p = jnp.exp2(s - pltpu.repeat(m_new, tk // LANES, axis=1))
        alpha = pltpu.repeat(jnp.exp2(m - m_new), 2, axis=1)
        v_aug = jnp.concatenate([vsl_ref[pl.ds(k0, tk), :], ones_k], axis=1)
        return m_new, alpha * acc + _dot(p.astype(BF16), v_aug)

    n_kt = jnp.maximum((qs + tk - 1) // tk, 1)
    put_scores(0, 0)

    def body(j, carry):
        kt = 2 * j
        s_cur, s_max = s_ref[0], smax_ref[0]
        put_scores(kt + 1, 1)
        carry = update(kt, s_cur, s_max, carry)
        s_cur, s_max = s_ref[1], smax_ref[1]
        put_scores(jnp.minimum(kt + 2, n_kt - 1), 0)
        return update(kt + 1, s_cur, s_max, carry)

    m_init = jnp.broadcast_to(m_own, (rows, LANES))
    carry = lax.fori_loop(0, n_kt // 2, body, (m_init, acc_own))
    _, acc_s = lax.cond(n_kt % 2 == 1,
                        lambda c: update(n_kt - 1, s_ref[0], smax_ref[0], c),
                        lambda c: c, carry)
    l_s = acc_s[:, HEAD_DIM:HEAD_DIM + 1]
    o_slc = acc_s[:, 0:HEAD_DIM] / jnp.where(l_s > 0.0, l_s, 1.0)

    gates = gate_ref[...]
    for r in range(GQA_GROUP):
        rs = slice(r * tq, (r + 1) * tq)
        cs = slice(r * HEAD_DIM, (r + 1) * HEAD_DIM)
        y_ref[:, cs] = (ocmp_ref[:, cs] + _gate_col(gates, g, r, 1) * o_slc[rs, :]
                        + _gate_col(gates, g, r, 2) * o_win[rs, :])


def _slc_win(q, ksl, vsl, kwn, vwn, sel, onehot, wbias, gates, ocmp):
    B, S, _ = q.shape
    tq = min(TQ, S)
    tk = min(TK_SLC, S)
    grp = GQA_GROUP * HEAD_DIM
    kernel = functools.partial(_slc_win_kernel, tq=tq, tk=tk)
    kv = pl.BlockSpec((None, S, HEAD_DIM), lambda b, g, i: (b, 0, g))
    return pl.pallas_call(
        kernel,
        grid=(B, N_KV_HEADS, S // tq),
        in_specs=[
            pl.BlockSpec((None, tq, grp), lambda b, g, i: (b, i, g)),
            kv, kv, kv, kv,
            pl.BlockSpec((None, None, tq, LANES), lambda b, g, i: (b, g, i, 0)),
            pl.BlockSpec((S, LANES), lambda b, g, i: (0, 0)),
            pl.BlockSpec((WINDOW + tq, tq), lambda b, g, i: (0, 0)),
            pl.BlockSpec((None, tq, LANES), lambda b, g, i: (b, i, 0)),
            pl.BlockSpec((None, tq, grp), lambda b, g, i: (b, i, g)),
        ],
        out_specs=pl.BlockSpec((None, tq, grp), lambda b, g, i: (b, i, g)),
        out_shape=jax.ShapeDtypeStruct((B, S, D_ATTN), F32),
        scratch_shapes=[pltpu.VMEM((2, GQA_GROUP * tq, tk), F32),
                        pltpu.VMEM((2, GQA_GROUP * tq, LANES), F32)],
        compiler_params=_params(("arbitrary", "arbitrary", "arbitrary")),
        name="slc_win",
    )(q, ksl, vsl, kwn, vwn, sel, onehot, wbias, gates, ocmp)


def _out_proj_kernel(yc_ref, ya_ref, x_ref, w_ref, ga_ref, g1_ref, n2_ref, sh_ref, sc_ref, x1_ref, h2_ref):
    ya = _rms(ya_ref[...], ga_ref[...]).astype(BF16)
    mix = _dot(yc_ref[...], w_ref[0:D_CONV, :]) + _dot(ya, w_ref[D_CONV:D_CONV + D_ATTN, :])
    x1 = x_ref[...] + g1_ref[...] * mix
    x1_ref[...] = x1
    h2_ref[...] = (_rms(x1, n2_ref[...]) * (1.0 + sc_ref[...]) + sh_ref[...]).astype(BF16)


def _out_proj(yconv, yattn, x, w_out_b, gattn_g, mod, norm2_g):
    B, S, D = x.shape
    tm = min(TM_PROJ, S)
    tok = lambda width: pl.BlockSpec((None, tm, width), lambda b, i: (b, i, 0))
    vec = lambda width: pl.BlockSpec((1, width), lambda b, i: (0, 0))
    modspec = lambda k: pl.BlockSpec((None, None, 1, D), lambda b, i, k=k: (b, k, 0, 0))
    return pl.pallas_call(
        _out_proj_kernel,
        grid=(B, S // tm),
        in_specs=[tok(D_CONV), tok(D_ATTN), tok(D),
                  pl.BlockSpec((D_CONV + D_ATTN, D), lambda b, i: (0, 0), pipeline_mode=pl.Buffered(1)),
                  vec(D_ATTN), modspec(2), vec(D), modspec(3), modspec(4)],
        out_specs=[tok(D), tok(D)],
        out_shape=[jax.ShapeDtypeStruct((B, S, D), F32), jax.ShapeDtypeStruct((B, S, D), BF16)],
        compiler_params=_params(("arbitrary", "arbitrary")),
        name="out_proj",
    )(yconv, yattn, x, w_out_b, gattn_g, mod, norm2_g, mod, mod)


def _ffn_kernel(h_ref, w1_ref, w2_ref, x1_ref, g2_ref, nf_ref, o_ref, acc_ref, *, final_norm):
    f = pl.program_id(2)

    @pl.when(f == 0)
    def _():
        acc_ref[...] = jnp.zeros(acc_ref.shape, F32)

    a = jnp.maximum(_dot(h_ref[...], w1_ref[...]), 0.0)
    acc_ref[...] += _dot((a * a).astype(BF16), w2_ref[...])

    @pl.when(f == pl.num_programs(2) - 1)
    def _():
        x2 = x1_ref[...] + g2_ref[...] * acc_ref[...]
        o_ref[...] = _rms(x2, nf_ref[...]) if final_norm else x2


def _ffn(h2, w1_b, w2_b, x1, mod, normf_g, final_norm):
    B, S, D = x1.shape
    d_ff = w1_b.shape[1]
    tm = min(TM_FFN, S)
    tf = min(TF_FFN, d_ff)
    tok = pl.BlockSpec((None, tm, D), lambda b, i, f: (b, i, 0))
    return pl.pallas_call(
        functools.partial(_ffn_kernel, final_norm=final_norm),
        grid=(B, S // tm, d_ff // tf),
        in_specs=[tok,
                  pl.BlockSpec((D, tf), lambda b, i, f: (0, f)),
                  pl.BlockSpec((tf, D), lambda b, i, f: (f, 0)),
                  tok,
                  pl.BlockSpec((None, None, 1, D), lambda b, i, f: (b, 5, 0, 0)),
                  pl.BlockSpec((1, D), lambda b, i, f: (0, 0))],
        out_specs=tok,
        out_shape=jax.ShapeDtypeStruct((B, S, D), F32),
        scratch_shapes=[pltpu.VMEM((tm, D), F32)],
        compiler_params=_params(("arbitrary", "arbitrary", "arbitrary")),
        name="ffn",
    )(h2, w1_b, w2_b, x1, mod, normf_g)


def _rope_tables(seq):
    inv = ROPE_THETA ** (-jnp.arange(0, HEAD_DIM, 2, dtype=F32) / HEAD_DIM)
    ang = jnp.arange(seq, dtype=F32)[:, None] * inv[None, :]
    cos, sin = jnp.cos(ang), jnp.sin(ang)
    return jnp.concatenate([cos, cos], axis=-1), jnp.concatenate([-sin, sin], axis=-1)


def _slc_from_cmp(n_blk):
    sj = np.arange(LANES)[:, None]
    ci = np.arange(n_blk)[None, :]
    m = (ci * CMP_STRIDE <= sj * SLC_LEN + SLC_LEN - 1) & (ci * CMP_STRIDE + CMP_LEN - 1 >= sj * SLC_LEN)
    return jnp.asarray(m, dtype=BF16)


def _block_onehot(seq):
    m = (np.arange(seq)[:, None] // SLC_LEN) == np.arange(LANES)[None, :]
    return jnp.asarray(m, dtype=BF16)


def _prefix_ones():
    return jnp.asarray(np.arange(LANES)[:, None] >= np.arange(LANES)[None, :], dtype=BF16)


def _window_bias(tq):
    key = np.arange(WINDOW + tq)[:, None]
    row = np.arange(tq)[None, :]
    ok = (key > row) & (key <= row + WINDOW)
    return jnp.asarray(np.where(ok, 0.0, MASK_BIAS), dtype=BF16)


def kernel(x, c, w_ada, b_ada, norm1_g, w_in, conv_w, conv_b, cmp_pe_k, cmp_pe_v, cmp_w1_k, cmp_w2_k,
           cmp_w1_v, cmp_w2_v, gnorm_conv_g, gnorm_attn_g, w_out, norm2_g, w_ff1, w_ff2, normf_g):
    B, S, D = x.shape
    depth = w_ada.shape[0]
    assert S % TQ == 0 and S // SLC_LEN <= LANES and S >= WINDOW + TQ
    assert w_in.shape[2] == D_IN
    cos_f, sin_f = _rope_tables(S)
    s2c = _slc_from_cmp(S // CMP_STRIDE)
    onehot = _block_onehot(S)
    tri = _prefix_ones()
    eye = jnp.eye(min(TQ_CMP, S), dtype=BF16)
    wbias = _window_bias(min(TQ, S))
    for l in range(depth):
        mod = _adaln(c, w_ada[l], b_ada[l]).reshape(B, 6, 1, D)
        w_in_b = jnp.pad(w_in[l].astype(BF16), ((0, 0), (0, D_IN_PAD - D_IN)))
        (yconv, q, kc, vc, ksl, vsl, kwn, vwn, gates) = _in_proj(
            x, mod, norm1_g[l][None], w_in_b, conv_w[l], conv_b[l][None], gnorm_conv_g[l][None], cos_f, sin_f)
        k_cmp = _compress(kc, cmp_pe_k[l], cmp_w1_k[l].astype(BF16), cmp_w2_k[l].astype(BF16))
        v_cmp = _compress(vc, cmp_pe_v[l], cmp_w1_v[l].astype(BF16), cmp_w2_v[l].astype(BF16))
        ocmp, sel = _cmp_select(q, k_cmp, v_cmp, s2c, tri, eye, gates)
        yattn = _slc_win(q, ksl, vsl, kwn, vwn, sel, onehot, wbias, gates, ocmp)
        x1, h2 = _out_proj(yconv, yattn, x, w_out[l].astype(BF16), gnorm_attn_g[l][None], mod, norm2_g[l][None])
        x = _ffn(h2, w_ff1[l].astype(BF16), w_ff2[l].astype(BF16), x1, mod, normf_g[None],
                 final_norm=(l == depth - 1))
    return x
```

```python
import functools

import numpy as np
import jax
import jax.numpy as jnp
from jax import lax
from jax.experimental import pallas as pl
from jax.experimental.pallas import tpu as pltpu

F32 = jnp.float32
BF16 = jnp.bfloat16

HEAD_DIM = 128
N_HEADS = 8
N_KV_HEADS = 2
GQA_GROUP = N_HEADS // N_KV_HEADS
D_CONV = 1024
D_ATTN = N_HEADS * HEAD_DIM
D_KV = N_KV_HEADS * HEAD_DIM
N_BRANCH = 3
CONV_WIDTH = 3
CMP_LEN = 32
CMP_STRIDE = 16
CMP_HIDDEN = 256
SLC_LEN = 64
N_SLC = 16
WINDOW = 512
ROPE_THETA = 10000.0
EPS = 1e-6
FORCE_BONUS = 1e4

LANES = 128
SUBLANES = 8
VMEM_LIMIT_BYTES = 56 * 1024 * 1024

NEG = float(np.finfo(np.float32).min)
MASK_BIAS = -(2.0 ** 126)
LOG2E = float(np.log2(np.e))

COL_UB, COL_UC, COL_UH = 0, D_CONV, 2 * D_CONV
COL_Q = 3 * D_CONV
COL_KV = COL_Q + D_ATTN
COL_GATE = COL_KV + 2 * N_BRANCH * D_KV
D_IN = COL_GATE + N_BRANCH * N_HEADS
D_IN_PAD = COL_GATE + LANES

TM_PROJ = 512
TQ = 128
TK_SLC = 1024
TM_FFN = 512
TF_FFN = 1024
TN_ADA = 1024


def _params(sem):
    return pltpu.CompilerParams(dimension_semantics=sem, vmem_limit_bytes=VMEM_LIMIT_BYTES)


def _dot(a, b):
    return jnp.dot(a, b, preferred_element_type=F32)


def _dot_nt(a, b):
    return lax.dot_general(a, b, (((1,), (1,)), ((), ())), preferred_element_type=F32)


def _rms(x, g):
    return x * lax.rsqrt(jnp.mean(x * x, axis=-1, keepdims=True) + EPS) * g


def _adaln_kernel(c_ref, w_ref, b_ref, o_ref):
    c = c_ref[...]
    s = c * jax.nn.sigmoid(c)
    o_ref[...] = _dot(s.astype(BF16), w_ref[...].astype(BF16)) + b_ref[...]


def _adaln(c, w_ada, b_ada):
    B, D = c.shape
    n_out = w_ada.shape[1]
    rows = -(-B // SUBLANES) * SUBLANES
    c_pad = jnp.zeros((rows, D), F32).at[:B].set(c)
    out = pl.pallas_call(
        _adaln_kernel,
        grid=(n_out // TN_ADA,),
        in_specs=[
            pl.BlockSpec((rows, D), lambda j: (0, 0)),
            pl.BlockSpec((D, TN_ADA), lambda j: (0, j)),
            pl.BlockSpec((1, TN_ADA), lambda j: (0, j)),
        ],
        out_specs=pl.BlockSpec((rows, TN_ADA), lambda j: (0, j)),
        out_shape=jax.ShapeDtypeStruct((rows, n_out), F32),
        compiler_params=_params(("arbitrary",)),
        name="adaln",
    )(c_pad, w_ada, b_ada.reshape(1, n_out))
    return out[:B]


def _rope(u, cos_f, sin_f):
    return u * cos_f + pltpu.roll(u, HEAD_DIM // 2, 1) * sin_f


def _in_proj_kernel(x_ref, sh_ref, sc_ref, g_ref, w_ref, cw_ref, cb_ref, gc_ref, cos_ref, sin_ref,
                    yconv_ref, q_ref, kc_ref, vc_ref, ksl_ref, vsl_ref, kwn_ref, vwn_ref, gate_ref,
                    vbuf_ref, *, tm, q_scale):
    @pl.when(pl.program_id(1) == 0)
    def _():
        vbuf_ref[tm:tm + SUBLANES, :] = jnp.zeros((SUBLANES, D_CONV), F32)

    x = x_ref[...]
    h = _rms(x, g_ref[...]) * (1.0 + sc_ref[...]) + sh_ref[...]
    hb = h.astype(BF16)

    ub = _dot(hb, w_ref[:, COL_UB:COL_UB + D_CONV])
    uc = _dot(hb, w_ref[:, COL_UC:COL_UC + D_CONV])
    uh = _dot(hb, w_ref[:, COL_UH:COL_UH + D_CONV])
    v = uc * uh
    vbuf_ref[0:SUBLANES, :] = vbuf_ref[tm:tm + SUBLANES, :]
    vbuf_ref[SUBLANES:SUBLANES + tm, :] = v
    v1 = vbuf_ref[SUBLANES - 1:SUBLANES - 1 + tm, :]
    v2 = vbuf_ref[SUBLANES - 2:SUBLANES - 2 + tm, :]
    z = cb_ref[...] + cw_ref[0:1, :] * v2 + cw_ref[1:2, :] * v1 + cw_ref[2:3, :] * v
    yconv_ref[...] = _rms(ub * z, gc_ref[...]).astype(BF16)

    cos_f = cos_ref[...]
    sin_f = sin_ref[...]

    uq = _dot(hb, w_ref[:, COL_Q:COL_Q + D_ATTN])
    for hd in range(N_HEADS):
        sl = slice(hd * HEAD_DIM, (hd + 1) * HEAD_DIM)
        q_ref[:, sl] = (_rope(uq[:, sl], cos_f, sin_f) * q_scale).astype(BF16)

    ukv = _dot(hb, w_ref[:, COL_KV:COL_GATE])
    outs = (kc_ref, vc_ref, ksl_ref, vsl_ref, kwn_ref, vwn_ref)
    for n, o_ref in enumerate(outs):
        for g in range(N_KV_HEADS):
            src = slice(n * D_KV + g * HEAD_DIM, n * D_KV + (g + 1) * HEAD_DIM)
            dst = slice(g * HEAD_DIM, (g + 1) * HEAD_DIM)
            t = ukv[:, src]
            if n % 2 == 0:
                t = _rope(t, cos_f, sin_f)
            o_ref[:, dst] = t.astype(o_ref.dtype)

    ug = _dot(hb, w_ref[:, COL_GATE:D_IN_PAD])
    gate_ref[...] = jax.nn.sigmoid(ug)


def _in_proj(x, mod, norm1_g, w_in_b, conv_w, conv_b, gconv_g, cos_f, sin_f):
    B, S, D = x.shape
    tm = min(TM_PROJ, S)
    tok = lambda width: pl.BlockSpec((None, tm, width), lambda b, i: (b, i, 0))
    vec = lambda width: pl.BlockSpec((1, width), lambda b, i: (0, 0))
    modspec = lambda k: pl.BlockSpec((None, None, 1, D), lambda b, i, k=k: (b, k, 0, 0))
    kernel = functools.partial(_in_proj_kernel, tm=tm, q_scale=LOG2E * HEAD_DIM ** -0.5)
    sd = jax.ShapeDtypeStruct
    return pl.pallas_call(
        kernel,
        grid=(B, S // tm),
        in_specs=[
            tok(D), modspec(0), modspec(1), vec(D),
            pl.BlockSpec((D, D_IN_PAD), lambda b, i: (0, 0), pipeline_mode=pl.Buffered(1)),
            pl.BlockSpec((CONV_WIDTH, D_CONV), lambda b, i: (0, 0)), vec(D_CONV), vec(D_CONV),
            pl.BlockSpec((tm, HEAD_DIM), lambda b, i: (i, 0)),
            pl.BlockSpec((tm, HEAD_DIM), lambda b, i: (i, 0)),
        ],
        out_specs=[tok(D_CONV), tok(D_ATTN), tok(D_KV), tok(D_KV), tok(D_KV), tok(D_KV), tok(D_KV),
                   tok(D_KV), tok(LANES)],
        out_shape=[sd((B, S, D_CONV), BF16), sd((B, S, D_ATTN), BF16),
                   sd((B, S, D_KV), F32), sd((B, S, D_KV), F32),
                   sd((B, S, D_KV), BF16), sd((B, S, D_KV), BF16),
                   sd((B, S, D_KV), BF16), sd((B, S, D_KV), BF16),
                   sd((B, S, LANES), F32)],
        scratch_shapes=[pltpu.VMEM((tm + 2 * SUBLANES, D_CONV), F32)],
        compiler_params=_params(("arbitrary", "arbitrary")),
        name="in_proj",
    )(x, mod, mod, norm1_g, w_in_b, conv_w, conv_b, gconv_g, cos_f, sin_f)


def _compress_kernel(kv_ref, pe_ref, w1_ref, w2_ref, o_ref, buf_ref, *, seq, n_blk):
    buf_ref[0:seq, :] = kv_ref[...]
    buf_ref[seq:seq + CMP_STRIDE, :] = jnp.zeros((CMP_STRIDE, HEAD_DIM), F32)
    acc = jnp.zeros((n_blk, CMP_HIDDEN), F32)
    for l in range(CMP_LEN):
        rows = buf_ref[pl.ds(l, n_blk, stride=CMP_STRIDE), :] + pe_ref[l:l + 1, :]
        acc = acc + _dot(rows.astype(BF16), w1_ref[l * HEAD_DIM:(l + 1) * HEAD_DIM, :])
    hid = jax.nn.gelu(acc)
    o_ref[...] = _dot(hid.astype(BF16), w2_ref[...]).astype(BF16)


def _compress(kv, pe, w1_b, w2_b):
    B, S, _ = kv.shape
    n_blk = S // CMP_STRIDE
    kernel = functools.partial(_compress_kernel, seq=S, n_blk=n_blk)
    return pl.pallas_call(
        kernel,
        grid=(B, N_KV_HEADS),
        in_specs=[
            pl.BlockSpec((None, S, HEAD_DIM), lambda b, g: (b, 0, g)),
            pl.BlockSpec((CMP_LEN, HEAD_DIM), lambda b, g: (0, 0)),
            pl.BlockSpec((CMP_LEN * HEAD_DIM, CMP_HIDDEN), lambda b, g: (0, 0)),
            pl.BlockSpec((CMP_HIDDEN, HEAD_DIM), lambda b, g: (0, 0)),
        ],
        out_specs=pl.BlockSpec((None, None, n_blk, HEAD_DIM), lambda b, g: (b, g, 0, 0)),
        out_shape=jax.ShapeDtypeStruct((B, N_KV_HEADS, n_blk, HEAD_DIM), BF16),
        scratch_shapes=[pltpu.VMEM((S + CMP_STRIDE, HEAD_DIM), F32)],
        compiler_params=_params(("arbitrary", "arbitrary")),
        name="compress",
    )(kv, pe, w1_b, w2_b)


def _stack_heads(q_ref):
    return jnp.concatenate([q_ref[:, r * HEAD_DIM:(r + 1) * HEAD_DIM] for r in range(GQA_GROUP)], axis=0)


def _gate_col(gates, g, r, branch):
    lo = r * N_BRANCH + branch
    hi = (GQA_GROUP + r) * N_BRANCH + branch
    return jnp.where(g == 0, gates[:, lo:lo + 1], gates[:, hi:hi + 1])


def _compressed_branch(q_band, kc_ref, vc_ref, cbias_ref, s2c_ref, tri_ref, eye_ref, qs, *, tq, n_blk):
    rows = GQA_GROUP * tq
    start = pl.multiple_of(n_blk - qs // CMP_STRIDE, SUBLANES)
    vis_bias = cbias_ref[pl.ds(start, n_blk), :].astype(BF16)
    s = _dot_nt(q_band, jnp.concatenate([kc_ref[...], vis_bias], axis=1))
    p = jnp.exp2(s - jnp.max(s, axis=-1, keepdims=True)).astype(BF16)
    ov = _dot(p, jnp.concatenate([vc_ref[...], jnp.ones((n_blk, HEAD_DIM), BF16)], axis=1))
    l = ov[:, HEAD_DIM:HEAD_DIM + 1]
    t_col = qs + (lax.broadcasted_iota(jnp.int32, (rows, 1), 0) & (tq - 1))
    o_cmp = jnp.where(t_col >= CMP_LEN - 1, ov[:, 0:HEAD_DIM] / jnp.where(l > 0.0, l, 1.0), 0.0)

    imp = jnp.zeros((LANES, tq), F32)
    for r in range(GQA_GROUP):
        it = _dot_nt(s2c_ref[...], p[r * tq:(r + 1) * tq, :])
        l_t = it[LANES:LANES + 1, :]
        imp = imp + it[0:LANES, :] / jnp.where(l_t > 0.0, l_t, 1.0)
    t = qs + lax.broadcasted_iota(jnp.int32, (LANES, tq), 1)
    imp = jnp.where(t >= CMP_LEN - 1, imp, 0.0)
    blk = lax.broadcasted_iota(jnp.int32, (LANES, tq), 0)
    cur = t // SLC_LEN
    valid = blk * SLC_LEN <= t
    forced = (blk == 0) | (blk == cur) | (blk == cur - 1)
    score = jnp.where(valid, imp + jnp.where(forced, FORCE_BONUS, 0.0), -1.0)

    x = score
    covered = jnp.zeros((1, tq), F32)
    n_before = jnp.zeros((1, tq), F32)
    theta = jnp.full((1, tq), -1.0, F32)
    for _ in range(N_SLC):
        mx = jnp.max(x, axis=0, keepdims=True)
        eq = x == mx
        cnt = jnp.sum(jnp.where(eq, 1.0, 0.0), axis=0, keepdims=True)
        crossing = (covered < float(N_SLC)) & (covered + cnt >= float(N_SLC))
        theta = jnp.where(crossing, mx, theta)
        n_before = jnp.where(crossing, covered, n_before)
        covered = covered + cnt
        x = jnp.where(eq, -jnp.inf, x)
    tie = score == theta
    tie_rank = _dot(tri_ref[...], jnp.where(tie, 1.0, 0.0).astype(BF16))
    chosen = (score > theta) | (tie & (tie_rank <= float(N_SLC) - n_before))
    sel_t = jnp.where(valid & chosen, 1.0, 0.0).astype(BF16)
    return o_cmp, _dot_nt(eye_ref[...], sel_t)


def _nsa_kernel(q_ref, kc_ref, vc_ref, cbias_ref, s2c_ref, tri_ref, eye_ref, ksl_ref, vsl_ref, kwn_ref, vwn_ref,
                onehot_ref, wbias_ref, gate_ref, y_ref, s_ref, smax_ref, *, tq, tk, n_blk):
    g = pl.program_id(1)
    qs = pl.program_id(2) * tq
    rows = GQA_GROUP * tq
    qst = _stack_heads(q_ref)
    ones_k = jnp.ones((tk, HEAD_DIM), BF16)
    ones_q = jnp.ones((tq, HEAD_DIM), BF16)
    row_id = lax.broadcasted_iota(jnp.int32, (rows, tq), 0) & (tq - 1)
    eye = jnp.where(row_id == lax.broadcasted_iota(jnp.int32, (rows, tq), 1), 1.0, 0.0).astype(BF16)
    q_band = jnp.concatenate([qst, eye], axis=1)

    n_chunk = WINDOW // tq + 1
    k_parts, v_parts = [], []
    for c in range(n_chunk):
        start = qs - WINDOW + c * tq
        src = pl.multiple_of(jnp.maximum(start, 0), tq)
        bias_c = wbias_ref[c * tq:(c + 1) * tq, :]
        bias_c = jnp.where(start >= 0, bias_c, jnp.full(bias_c.shape, MASK_BIAS, BF16))
        k_parts.append(jnp.concatenate([kwn_ref[pl.ds(src, tq), :], bias_c], axis=1))
        v_parts.append(jnp.concatenate([vwn_ref[pl.ds(src, tq), :], ones_q], axis=1))
    own = pl.ds(pl.multiple_of(qs, tq), tq)
    causal_bias = wbias_ref[(n_chunk - 1) * tq:n_chunk * tq, :]
    k_parts.append(jnp.concatenate([ksl_ref[own, :], causal_bias], axis=1))
    s_band = _dot_nt(q_band, jnp.concatenate(k_parts, axis=0))

    o_cmp, sel = _compressed_branch(q_band, kc_ref, vc_ref, cbias_ref, s2c_ref, tri_ref, eye_ref, qs,
                                    tq=tq, n_blk=n_blk)

    s_w = s_band[:, 0:WINDOW + tq]
    p_w = jnp.exp2(s_w - jnp.max(s_w, axis=-1, keepdims=True))
    acc_w = _dot(p_w.astype(BF16), jnp.concatenate(v_parts, axis=0))
    l_w = acc_w[:, HEAD_DIM:HEAD_DIM + 1]
    o_win = acc_w[:, 0:HEAD_DIM] / jnp.where(l_w > 0.0, l_w, 1.0)
    s_own = s_band[:, WINDOW + tq:WINDOW + 2 * tq]
    m_own = jnp.max(s_own, axis=-1, keepdims=True)
    acc_own = _dot(jnp.exp2(s_own - m_own).astype(BF16), jnp.concatenate([vsl_ref[own, :], ones_q], axis=1))

    blk = lax.broadcasted_iota(jnp.int32, (tq, LANES), 1)
    sel_bias = (sel - 1.0) * (-MASK_BIAS)
    sel_bias = jnp.where(blk < qs // SLC_LEN, sel_bias, MASK_BIAS).astype(BF16)
    q_slc = jnp.concatenate([qst, jnp.concatenate([sel_bias] * GQA_GROUP, axis=0)], axis=1)

    def put_scores(kt, slot):
        k0 = pl.multiple_of(kt * tk, tk)
        k_aug = jnp.concatenate([ksl_ref[pl.ds(k0, tk), :], onehot_ref[pl.ds(k0, tk), :]], axis=1)
        s = _dot_nt(q_slc, k_aug)
        s_ref[slot] = s
        smax_ref[slot] = jnp.broadcast_to(jnp.max(s, axis=-1, keepdims=True), (rows, LANES))

    def update(kt, s, s_max, carry):
        m, acc = carry
        k0 = pl.multiple_of(kt * tk, tk)
        m_new = jnp.maximum(m, s_max)
        p = jnp.exp2(s - jnp.concatenate([m_new] * (tk // LANES), axis=1))
        alpha = jnp.concatenate([jnp.exp2(m - m_new)] * 2, axis=1)
        v_aug = jnp.concatenate([vsl_ref[pl.ds(k0, tk), :], ones_k], axis=1)
        return m_new, alpha * acc + _dot(p.astype(BF16), v_aug)

    n_kt = jnp.maximum((qs + tk - 1) // tk, 1)
    put_scores(0, 0)

    def body(j, carry):
        kt = 2 * j
        s_cur, s_max = s_ref[0], smax_ref[0]
        put_scores(kt + 1, 1)
        carry = update(kt, s_cur, s_max, carry)
        s_cur, s_max = s_ref[1], smax_ref[1]
        put_scores(jnp.minimum(kt + 2, n_kt - 1), 0)
        return update(kt + 1, s_cur, s_max, carry)

    m_init = jnp.broadcast_to(m_own, (rows, LANES))
    carry = lax.fori_loop(0, n_kt // 2, body, (m_init, acc_own))
    _, acc_s = lax.cond(n_kt % 2 == 1,
                        lambda c: update(n_kt - 1, s_ref[0], smax_ref[0], c),
                        lambda c: c, carry)
    l_s = acc_s[:, HEAD_DIM:HEAD_DIM + 1]
    o_slc = acc_s[:, 0:HEAD_DIM] / jnp.where(l_s > 0.0, l_s, 1.0)

    gates = gate_ref[...]
    for r in range(GQA_GROUP):
        rs = slice(r * tq, (r + 1) * tq)
        cs = slice(r * HEAD_DIM, (r + 1) * HEAD_DIM)
        y_ref[:, cs] = (_gate_col(gates, g, r, 0) * o_cmp[rs, :] + _gate_col(gates, g, r, 1) * o_slc[rs, :]
                        + _gate_col(gates, g, r, 2) * o_win[rs, :])


def _nsa_attention(q, k_cmp, v_cmp, ksl, vsl, kwn, vwn, cbias, s2c, tri, eye, onehot, wbias, gates):
    B, S, _ = q.shape
    n_blk = k_cmp.shape[2]
    tq = min(TQ, S)
    tk = min(TK_SLC, S)
    grp = GQA_GROUP * HEAD_DIM
    kernel = functools.partial(_nsa_kernel, tq=tq, tk=tk, n_blk=n_blk)
    kv = pl.BlockSpec((None, S, HEAD_DIM), lambda b, g, i: (b, 0, g))
    cmp_kv = pl.BlockSpec((None, None, n_blk, HEAD_DIM), lambda b, g, i: (b, g, 0, 0))
    const = lambda shape: pl.BlockSpec(shape, lambda b, g, i: (0, 0))
    return pl.pallas_call(
        kernel,
        grid=(B, N_KV_HEADS, S // tq),
        in_specs=[
            pl.BlockSpec((None, tq, grp), lambda b, g, i: (b, i, g)),
            cmp_kv, cmp_kv, const(cbias.shape), const(s2c.shape), const((LANES, LANES)), const((tq, tq)),
            kv, kv, kv, kv,
            const((S, LANES)), const((WINDOW + tq, tq)),
            pl.BlockSpec((None, tq, LANES), lambda b, g, i: (b, i, 0)),
        ],
        out_specs=pl.BlockSpec((None, tq, grp), lambda b, g, i: (b, i, g)),
        out_shape=jax.ShapeDtypeStruct((B, S, D_ATTN), F32),
        scratch_shapes=[pltpu.VMEM((2, GQA_GROUP * tq, tk), F32),
                        pltpu.VMEM((2, GQA_GROUP * tq, LANES), F32)],
        compiler_params=_params(("arbitrary", "arbitrary", "arbitrary")),
        name="nsa_attention",
    )(q, k_cmp, v_cmp, cbias, s2c, tri, eye, ksl, vsl, kwn, vwn, onehot, wbias, gates)


def _out_proj_kernel(yc_ref, ya_ref, x_ref, w_ref, ga_ref, g1_ref, n2_ref, sh_ref, sc_ref, x1_ref, h2_ref):
    ya = _rms(ya_ref[...], ga_ref[...]).astype(BF16)
    mix = _dot(yc_ref[...], w_ref[0:D_CONV, :]) + _dot(ya, w_ref[D_CONV:D_CONV + D_ATTN, :])
    x1 = x_ref[...] + g1_ref[...] * mix
    x1_ref[...] = x1
    h2_ref[...] = (_rms(x1, n2_ref[...]) * (1.0 + sc_ref[...]) + sh_ref[...]).astype(BF16)


def _out_proj(yconv, yattn, x, w_out_b, gattn_g, mod, norm2_g):
    B, S, D = x.shape
    tm = min(TM_PROJ, S)
    tok = lambda width: pl.BlockSpec((None, tm, width), lambda b, i: (b, i, 0))
    vec = lambda width: pl.BlockSpec((1, width), lambda b, i: (0, 0))
    modspec = lambda k: pl.BlockSpec((None, None, 1, D), lambda b, i, k=k: (b, k, 0, 0))
    return pl.pallas_call(
        _out_proj_kernel,
        grid=(B, S // tm),
        in_specs=[tok(D_CONV), tok(D_ATTN), tok(D),
                  pl.BlockSpec((D_CONV + D_ATTN, D), lambda b, i: (0, 0), pipeline_mode=pl.Buffered(1)),
                  vec(D_ATTN), modspec(2), vec(D), modspec(3), modspec(4)],
        out_specs=[tok(D), tok(D)],
        out_shape=[jax.ShapeDtypeStruct((B, S, D), F32), jax.ShapeDtypeStruct((B, S, D), BF16)],
        compiler_params=_params(("arbitrary", "arbitrary")),
        name="out_proj",
    )(yconv, yattn, x, w_out_b, gattn_g, mod, norm2_g, mod, mod)


def _ffn_kernel(h_ref, w1_ref, w2_ref, x1_ref, g2_ref, nf_ref, o_ref, acc_ref, *, final_norm):
    f = pl.program_id(2)

    @pl.when(f == 0)
    def _():
        acc_ref[...] = jnp.zeros(acc_ref.shape, F32)

    a = jnp.maximum(_dot(h_ref[...], w1_ref[...]), 0.0)
    acc_ref[...] += _dot((a * a).astype(BF16), w2_ref[...])

    @pl.when(f == pl.num_programs(2) - 1)
    def _():
        x2 = x1_ref[...] + g2_ref[...] * acc_ref[...]
        o_ref[...] = _rms(x2, nf_ref[...]) if final_norm else x2


def _ffn(h2, w1_b, w2_b, x1, mod, normf_g, final_norm):
    B, S, D = x1.shape
    d_ff = w1_b.shape[1]
    tm = min(TM_FFN, S)
    tf = min(TF_FFN, d_ff)
    tok = pl.BlockSpec((None, tm, D), lambda b, i, f: (b, i, 0))
    return pl.pallas_call(
        functools.partial(_ffn_kernel, final_norm=final_norm),
        grid=(B, S // tm, d_ff // tf),
        in_specs=[tok,
                  pl.BlockSpec((D, tf), lambda b, i, f: (0, f)),
                  pl.BlockSpec((tf, D), lambda b, i, f: (f, 0)),
                  tok,
                  pl.BlockSpec((None, None, 1, D), lambda b, i, f: (b, 5, 0, 0)),
                  pl.BlockSpec((1, D), lambda b, i, f: (0, 0))],
        out_specs=tok,
        out_shape=jax.ShapeDtypeStruct((B, S, D), F32),
        scratch_shapes=[pltpu.VMEM((tm, D), F32)],
        compiler_params=_params(("arbitrary", "arbitrary", "arbitrary")),
        name="ffn",
    )(h2, w1_b, w2_b, x1, mod, normf_g)


def _rope_tables(seq):
    inv = ROPE_THETA ** (-jnp.arange(0, HEAD_DIM, 2, dtype=F32) / HEAD_DIM)
    ang = jnp.arange(seq, dtype=F32)[:, None] * inv[None, :]
    cos, sin = jnp.cos(ang), jnp.sin(ang)
    return jnp.concatenate([cos, cos], axis=-1), jnp.concatenate([-sin, sin], axis=-1)


def _slc_from_cmp(n_blk):
    sj = np.arange(LANES)[:, None]
    ci = np.arange(n_blk)[None, :]
    m = (ci * CMP_STRIDE <= sj * SLC_LEN + SLC_LEN - 1) & (ci * CMP_STRIDE + CMP_LEN - 1 >= sj * SLC_LEN)
    return jnp.asarray(np.concatenate([m, np.ones((2 * SUBLANES, n_blk), bool)], axis=0), dtype=BF16)


def _cmp_visibility_bias(n_blk, tq):
    d = np.arange(2 * n_blk)[:, None] - n_blk
    r = np.arange(tq)[None, :]
    return jnp.asarray(np.where(CMP_STRIDE * d + CMP_LEN - 1 <= r, 0.0, MASK_BIAS), dtype=F32)


def _block_onehot(seq):
    m = (np.arange(seq)[:, None] // SLC_LEN) == np.arange(LANES)[None, :]
    return jnp.asarray(m, dtype=BF16)


def _prefix_ones():
    return jnp.asarray(np.arange(LANES)[:, None] >= np.arange(LANES)[None, :], dtype=BF16)


def _window_bias(tq):
    key = np.arange(WINDOW + tq)[:, None]
    row = np.arange(tq)[None, :]
    ok = (key > row) & (key <= row + WINDOW)
    return jnp.asarray(np.where(ok, 0.0, MASK_BIAS), dtype=BF16)


def kernel(x, c, w_ada, b_ada, norm1_g, w_in, conv_w, conv_b, cmp_pe_k, cmp_pe_v, cmp_w1_k, cmp_w2_k,
           cmp_w1_v, cmp_w2_v, gnorm_conv_g, gnorm_attn_g, w_out, norm2_g, w_ff1, w_ff2, normf_g):
    B, S, D = x.shape
    depth = w_ada.shape[0]
    assert S % TQ == 0 and S // SLC_LEN <= LANES and S >= WINDOW + TQ
    assert w_in.shape[2] == D_IN
    cos_f, sin_f = _rope_tables(S)
    s2c = _slc_from_cmp(S // CMP_STRIDE)
    cbias = _cmp_visibility_bias(S // CMP_STRIDE, min(TQ, S))
    onehot = _block_onehot(S)
    tri = _prefix_ones()
    eye = jnp.eye(min(TQ, S), dtype=BF16)
    wbias = _window_bias(min(TQ, S))
    for l in range(depth):
        mod = _adaln(c, w_ada[l], b_ada[l]).reshape(B, 6, 1, D)
        w_in_b = jnp.pad(w_in[l].astype(BF16), ((0, 0), (0, D_IN_PAD - D_IN)))
        (yconv, q, kc, vc, ksl, vsl, kwn, vwn, gates) = _in_proj(
            x, mod, norm1_g[l][None], w_in_b, conv_w[l], conv_b[l][None], gnorm_conv_g[l][None], cos_f, sin_f)
        k_cmp = _compress(kc, cmp_pe_k[l], cmp_w1_k[l].astype(BF16), cmp_w2_k[l].astype(BF16))
        v_cmp = _compress(vc, cmp_pe_v[l], cmp_w1_v[l].astype(BF16), cmp_w2_v[l].astype(BF16))
        yattn = _nsa_attention(q, k_cmp, v_cmp, ksl, vsl, kwn, vwn, cbias, s2c, tri, eye, onehot, wbias, gates)
        x1, h2 = _out_proj(yconv, yattn, x, w_out[l].astype(BF16), gnorm_attn_g[l][None], mod, norm2_g[l][None])
        x = _ffn(h2, w_ff1[l].astype(BF16), w_ff2[l].astype(BF16), x1, mod, normf_g[None],
                 final_norm=(l == depth - 1))
    return x
```

```python
import functools

import numpy as np
import jax
import jax.numpy as jnp
from jax import lax
from jax.experimental import pallas as pl
from jax.experimental.pallas import tpu as pltpu

F32 = jnp.float32
BF16 = jnp.bfloat16

HEAD_DIM = 128
N_HEADS = 8
N_KV_HEADS = 2
GQA_GROUP = N_HEADS // N_KV_HEADS
D_CONV = 1024
D_ATTN = N_HEADS * HEAD_DIM
D_KV = N_KV_HEADS * HEAD_DIM
N_BRANCH = 3
CONV_WIDTH = 3
CMP_LEN = 32
CMP_STRIDE = 16
CMP_HIDDEN = 256
SLC_LEN = 64
N_SLC = 16
WINDOW = 512
ROPE_THETA = 10000.0
EPS = 1e-6
FORCE_BONUS = 1e4

LANES = 128
SUBLANES = 8
VMEM_LIMIT_BYTES = 56 * 1024 * 1024

NEG = float(np.finfo(np.float32).min)
MASK_BIAS = -(2.0 ** 126)
LOG2E = float(np.log2(np.e))

COL_UB, COL_UC, COL_UH = 0, D_CONV, 2 * D_CONV
COL_Q = 3 * D_CONV
COL_KV = COL_Q + D_ATTN
COL_GATE = COL_KV + 2 * N_BRANCH * D_KV
D_IN = COL_GATE + N_BRANCH * N_HEADS
D_IN_PAD = COL_GATE + LANES

TM_PROJ = 512
TQ = 128
TK_SLC = 1024
TM_FFN = 512
TF_FFN = 1024
TN_ADA = 1024


def _params(sem):
    return pltpu.CompilerParams(dimension_semantics=sem, vmem_limit_bytes=VMEM_LIMIT_BYTES)


def _dot(a, b):
    return jnp.dot(a, b, preferred_element_type=F32)


def _dot_nt(a, b):
    return lax.dot_general(a, b, (((1,), (1,)), ((), ())), preferred_element_type=F32)


def _rms(x, g):
    return x * lax.rsqrt(jnp.mean(x * x, axis=-1, keepdims=True) + EPS) * g


def _adaln_kernel(c_ref, w_ref, b_ref, o_ref):
    c = c_ref[...]
    s = c * jax.nn.sigmoid(c)
    o_ref[...] = _dot(s.astype(BF16), w_ref[...].astype(BF16)) + b_ref[...]


def _adaln(c, w_ada, b_ada):
    B, D = c.shape
    n_out = w_ada.shape[1]
    rows = -(-B // SUBLANES) * SUBLANES
    c_pad = jnp.zeros((rows, D), F32).at[:B].set(c)
    out = pl.pallas_call(
        _adaln_kernel,
        grid=(n_out // TN_ADA,),
        in_specs=[
            pl.BlockSpec((rows, D), lambda j: (0, 0)),
            pl.BlockSpec((D, TN_ADA), lambda j: (0, j)),
            pl.BlockSpec((1, TN_ADA), lambda j: (0, j)),
        ],
        out_specs=pl.BlockSpec((rows, TN_ADA), lambda j: (0, j)),
        out_shape=jax.ShapeDtypeStruct((rows, n_out), F32),
        compiler_params=_params(("arbitrary",)),
        name="adaln",
    )(c_pad, w_ada, b_ada.reshape(1, n_out))
    return out[:B]


def _rope(u, cos_f, sin_f):
    return u * cos_f + pltpu.roll(u, HEAD_DIM // 2, 1) * sin_f


def _in_proj_kernel(x_ref, sh_ref, sc_ref, g_ref, w_ref, cw_ref, cb_ref, gc_ref, cos_ref, sin_ref,
                    yconv_ref, q_ref, kc_ref, vc_ref, ksl_ref, vsl_ref, kwn_ref, vwn_ref, gate_ref,
                    vbuf_ref, *, tm, q_scale):
    @pl.when(pl.program_id(1) == 0)
    def _():
        vbuf_ref[tm:tm + SUBLANES, :] = jnp.zeros((SUBLANES, D_CONV), F32)

    x = x_ref[...]
    h = _rms(x, g_ref[...]) * (1.0 + sc_ref[...]) + sh_ref[...]
    hb = h.astype(BF16)

    ub = _dot(hb, w_ref[:, COL_UB:COL_UB + D_CONV])
    uc = _dot(hb, w_ref[:, COL_UC:COL_UC + D_CONV])
    uh = _dot(hb, w_ref[:, COL_UH:COL_UH + D_CONV])
    v = uc * uh
    vbuf_ref[0:SUBLANES, :] = vbuf_ref[tm:tm + SUBLANES, :]
    vbuf_ref[SUBLANES:SUBLANES + tm, :] = v
    v1 = vbuf_ref[SUBLANES - 1:SUBLANES - 1 + tm, :]
    v2 = vbuf_ref[SUBLANES - 2:SUBLANES - 2 + tm, :]
    z = cb_ref[...] + cw_ref[0:1, :] * v2 + cw_ref[1:2, :] * v1 + cw_ref[2:3, :] * v
    yconv_ref[...] = _rms(ub * z, gc_ref[...]).astype(BF16)

    cos_f = cos_ref[...]
    sin_f = sin_ref[...]

    uq = _dot(hb, w_ref[:, COL_Q:COL_Q + D_ATTN])
    for hd in range(N_HEADS):
        sl = slice(hd * HEAD_DIM, (hd + 1) * HEAD_DIM)
        q_ref[:, sl] = (_rope(uq[:, sl], cos_f, sin_f) * q_scale).astype(BF16)

    ukv = _dot(hb, w_ref[:, COL_KV:COL_GATE])
    outs = (kc_ref, vc_ref, ksl_ref, vsl_ref, kwn_ref, vwn_ref)
    for n, o_ref in enumerate(outs):
        for g in range(N_KV_HEADS):
            src = slice(n * D_KV + g * HEAD_DIM, n * D_KV + (g + 1) * HEAD_DIM)
            dst = slice(g * HEAD_DIM, (g + 1) * HEAD_DIM)
            t = ukv[:, src]
            if n % 2 == 0:
                t = _rope(t, cos_f, sin_f)
            o_ref[:, dst] = t.astype(o_ref.dtype)

    ug = _dot(hb, w_ref[:, COL_GATE:D_IN_PAD])
    gate_ref[...] = jax.nn.sigmoid(ug)


def _in_proj(x, mod, norm1_g, w_in_b, conv_w, conv_b, gconv_g, cos_f, sin_f):
    B, S, D = x.shape
    tm = min(TM_PROJ, S)
    tok = lambda width: pl.BlockSpec((None, tm, width), lambda b, i: (b, i, 0))
    vec = lambda width: pl.BlockSpec((1, width), lambda b, i: (0, 0))
    modspec = lambda k: pl.BlockSpec((None, None, 1, D), lambda b, i, k=k: (b, k, 0, 0))
    kernel = functools.partial(_in_proj_kernel, tm=tm, q_scale=LOG2E * HEAD_DIM ** -0.5)
    sd = jax.ShapeDtypeStruct
    return pl.pallas_call(
        kernel,
        grid=(B, S // tm),
        in_specs=[
            tok(D), modspec(0), modspec(1), vec(D),
            pl.BlockSpec((D, D_IN_PAD), lambda b, i: (0, 0), pipeline_mode=pl.Buffered(1)),
            pl.BlockSpec((CONV_WIDTH, D_CONV), lambda b, i: (0, 0)), vec(D_CONV), vec(D_CONV),
            pl.BlockSpec((tm, HEAD_DIM), lambda b, i: (i, 0)),
            pl.BlockSpec((tm, HEAD_DIM), lambda b, i: (i, 0)),
        ],
        out_specs=[tok(D_CONV), tok(D_ATTN), tok(D_KV), tok(D_KV), tok(D_KV), tok(D_KV), tok(D_KV),
                   tok(D_KV), tok(LANES)],
        out_shape=[sd((B, S, D_CONV), BF16), sd((B, S, D_ATTN), BF16),
                   sd((B, S, D_KV), F32), sd((B, S, D_KV), F32),
                   sd((B, S, D_KV), BF16), sd((B, S, D_KV), BF16),
                   sd((B, S, D_KV), BF16), sd((B, S, D_KV), BF16),
                   sd((B, S, LANES), F32)],
        scratch_shapes=[pltpu.VMEM((tm + 2 * SUBLANES, D_CONV), F32)],
        compiler_params=_params(("arbitrary", "arbitrary")),
        name="in_proj",
    )(x, mod, mod, norm1_g, w_in_b, conv_w, conv_b, gconv_g, cos_f, sin_f)


def _compress_kernel(kv_ref, pe_ref, w1_ref, w2_ref, o_ref, buf_ref, *, seq, n_blk):
    buf_ref[0:seq, :] = kv_ref[...]
    buf_ref[seq:seq + CMP_STRIDE, :] = jnp.zeros((CMP_STRIDE, HEAD_DIM), F32)
    acc = jnp.zeros((n_blk, CMP_HIDDEN), F32)
    for l in range(CMP_LEN):
        rows = buf_ref[pl.ds(l, n_blk, stride=CMP_STRIDE), :] + pe_ref[l:l + 1, :]
        acc = acc + _dot(rows.astype(BF16), w1_ref[l * HEAD_DIM:(l + 1) * HEAD_DIM, :])
    hid = jax.nn.gelu(acc)
    o_ref[...] = _dot(hid.astype(BF16), w2_ref[...]).astype(BF16)


def _compress(kv, pe, w1_b, w2_b):
    B, S, _ = kv.shape
    n_blk = S // CMP_STRIDE
    kernel = functools.partial(_compress_kernel, seq=S, n_blk=n_blk)
    return pl.pallas_call(
        kernel,
        grid=(B, N_KV_HEADS),
        in_specs=[
            pl.BlockSpec((None, S, HEAD_DIM), lambda b, g: (b, 0, g)),
            pl.BlockSpec((CMP_LEN, HEAD_DIM), lambda b, g: (0, 0)),
            pl.BlockSpec((CMP_LEN * HEAD_DIM, CMP_HIDDEN), lambda b, g: (0, 0)),
            pl.BlockSpec((CMP_HIDDEN, HEAD_DIM), lambda b, g: (0, 0)),
        ],
        out_specs=pl.BlockSpec((None, None, n_blk, HEAD_DIM), lambda b, g: (b, g, 0, 0)),
        out_shape=jax.ShapeDtypeStruct((B, N_KV_HEADS, n_blk, HEAD_DIM), BF16),
        scratch_shapes=[pltpu.VMEM((S + CMP_STRIDE, HEAD_DIM), F32)],
        compiler_params=_params(("arbitrary", "arbitrary")),
        name="compress",
    )(kv, pe, w1_b, w2_b)


def _stack_heads(q_ref):
    return jnp.concatenate([q_ref[:, r * HEAD_DIM:(r + 1) * HEAD_DIM] for r in range(GQA_GROUP)], axis=0)


def _normalized(acc):
    l = acc[:, HEAD_DIM:2 * HEAD_DIM]
    return acc[:, 0:HEAD_DIM] / jnp.where(l > 0.0, l, 1.0)


def _gate_col(gates, g, r, branch):
    lo = r * N_BRANCH + branch
    hi = (GQA_GROUP + r) * N_BRANCH + branch
    return jnp.where(g == 0, gates[:, lo:lo + 1], gates[:, hi:hi + 1])


def _cmp_scores(q_band, kc_ref, cbias_ref, qs, *, n_blk):
    start = pl.multiple_of(n_blk - qs // CMP_STRIDE, SUBLANES)
    vis_bias = cbias_ref[pl.ds(start, n_blk), :].astype(BF16)
    return _dot_nt(q_band, jnp.concatenate([kc_ref[...], vis_bias], axis=1))


def _cmp_attend(s, vc_ref, s2c_ref, qs, *, tq, n_blk):
    rows = GQA_GROUP * tq
    p = jnp.exp2(s - jnp.max(s, axis=-1, keepdims=True)).astype(BF16)
    ov = _dot(p, jnp.concatenate([vc_ref[...], jnp.ones((n_blk, HEAD_DIM), BF16)], axis=1))
    t_row = qs + (lax.broadcasted_iota(jnp.int32, (rows, HEAD_DIM), 0) & (tq - 1))
    o_cmp = jnp.where(t_row >= CMP_LEN - 1, _normalized(ov), 0.0)

    imp = jnp.zeros((LANES, tq), F32)
    for r in range(GQA_GROUP):
        it = _dot_nt(s2c_ref[...], p[r * tq:(r + 1) * tq, :])
        l_t = it[LANES:LANES + 1, :]
        imp = imp + it[0:LANES, :] / jnp.where(l_t > 0.0, l_t, 1.0)
    t = qs + lax.broadcasted_iota(jnp.int32, (LANES, tq), 1)
    imp = jnp.where(t >= CMP_LEN - 1, imp, 0.0)
    blk = lax.broadcasted_iota(jnp.int32, (LANES, tq), 0)
    cur = t // SLC_LEN
    valid = blk * SLC_LEN <= t
    forced = (blk == 0) | (blk == cur) | (blk == cur - 1)
    return o_cmp, jnp.where(valid, imp + jnp.where(forced, FORCE_BONUS, 0.0), -1.0), valid


def _cmp_select(score, valid, tri_ref, eye_ref, qs, *, tq):
    x = score
    covered = jnp.zeros((1, tq), F32)
    n_before = jnp.zeros((1, tq), F32)
    theta = jnp.full((1, tq), -1.0, F32)
    for _ in range(N_SLC):
        mx = jnp.max(x, axis=0, keepdims=True)
        eq = x == mx
        cnt = jnp.sum(jnp.where(eq, 1.0, 0.0), axis=0, keepdims=True)
        crossing = (covered < float(N_SLC)) & (covered + cnt >= float(N_SLC))
        theta = jnp.where(crossing, mx, theta)
        n_before = jnp.where(crossing, covered, n_before)
        covered = covered + cnt
        x = jnp.where(eq, -jnp.inf, x)
    tie = score == theta
    tie_rank = _dot(tri_ref[...], jnp.where(tie, 1.0, 0.0).astype(BF16))
    chosen = (score > theta) | (tie & (tie_rank <= float(N_SLC) - n_before))
    sel_t = jnp.where(valid & chosen, 1.0, 0.0).astype(BF16)
    sel = _dot_nt(eye_ref[...], sel_t)
    below = lax.broadcasted_iota(jnp.int32, (tq, LANES), 1) < qs // SLC_LEN
    return jnp.where(below, (sel - 1.0) * (-MASK_BIAS), MASK_BIAS).astype(BF16)


def _nsa_kernel(q_ref, qn_ref, kc_ref, vc_ref, cbias_ref, s2c_ref, tri_ref, eye_ref, ksl_ref, vsl_ref, kwn_ref,
                vwn_ref, onehot_ref, wbias_ref, gates_in_ref, y_ref, s_ref, smax_ref, ocmp_ref, selb_ref, gate_ref,
                *, tq, tk, n_blk):
    g = pl.program_id(1)
    i = pl.program_id(2)
    qs = i * tq
    slot = i & 1
    rows = GQA_GROUP * tq
    row_id = lax.broadcasted_iota(jnp.int32, (rows, tq), 0) & (tq - 1)
    eye = jnp.where(row_id == lax.broadcasted_iota(jnp.int32, (rows, tq), 1), 1.0, 0.0).astype(BF16)

    @pl.when(i == 0)
    def _():
        q0 = jnp.concatenate([_stack_heads(q_ref), eye], axis=1)
        s0 = _cmp_scores(q0, kc_ref, cbias_ref, qs, n_blk=n_blk)
        ocmp_ref[0], score0, valid0 = _cmp_attend(s0, vc_ref, s2c_ref, qs, tq=tq, n_blk=n_blk)
        selb_ref[0] = _cmp_select(score0, valid0, tri_ref, eye_ref, qs, tq=tq)

    qst = _stack_heads(q_ref)
    ones_k = jnp.ones((tk, HEAD_DIM), BF16)
    ones_q = jnp.ones((tq, HEAD_DIM), BF16)
    q_band = jnp.concatenate([qst, eye], axis=1)
    sel_bias = selb_ref[slot]
    gates = gates_in_ref[...]
    for r in range(GQA_GROUP):
        for br in range(N_BRANCH):
            gate_ref[r * N_BRANCH + br] = jnp.broadcast_to(_gate_col(gates, g, r, br), (tq, HEAD_DIM))

    qs_next = qs + tq
    q_next = jnp.concatenate([_stack_heads(qn_ref), eye], axis=1)
    s_next = _cmp_scores(q_next, kc_ref, cbias_ref, qs_next, n_blk=n_blk)

    n_chunk = WINDOW // tq + 1
    k_parts, v_parts = [], []
    for c in range(n_chunk):
        start = qs - WINDOW + c * tq
        src = pl.multiple_of(jnp.maximum(start, 0), tq)
        bias_c = wbias_ref[c * tq:(c + 1) * tq, :]
        bias_c = jnp.where(start >= 0, bias_c, jnp.full(bias_c.shape, MASK_BIAS, BF16))
        k_parts.append(jnp.concatenate([kwn_ref[pl.ds(src, tq), :], bias_c], axis=1))
        v_parts.append(jnp.concatenate([vwn_ref[pl.ds(src, tq), :], ones_q], axis=1))
    own = pl.ds(pl.multiple_of(qs, tq), tq)
    causal_bias = wbias_ref[(n_chunk - 1) * tq:n_chunk * tq, :]
    k_parts.append(jnp.concatenate([ksl_ref[own, :], causal_bias], axis=1))
    s_band = _dot_nt(q_band, jnp.concatenate(k_parts, axis=0))

    ocmp_ref[1 - slot], score_next, valid_next = _cmp_attend(s_next, vc_ref, s2c_ref, qs_next, tq=tq, n_blk=n_blk)

    q_slc = jnp.concatenate([qst, jnp.concatenate([sel_bias] * GQA_GROUP, axis=0)], axis=1)

    def put_scores(kt, slot):
        k0 = pl.multiple_of(kt * tk, tk)
        k_aug = jnp.concatenate([ksl_ref[pl.ds(k0, tk), :], onehot_ref[pl.ds(k0, tk), :]], axis=1)
        s = _dot_nt(q_slc, k_aug)
        s_ref[slot] = s
        smax_ref[slot] = jnp.broadcast_to(jnp.max(s, axis=-1, keepdims=True), (rows, LANES))

    put_scores(0, 0)

    selb_ref[1 - slot] = _cmp_select(score_next, valid_next, tri_ref, eye_ref, qs_next, tq=tq)

    s_w = s_band[:, 0:WINDOW + tq]
    p_w = jnp.exp2(s_w - jnp.max(s_w, axis=-1, keepdims=True))
    acc_w = _dot(p_w.astype(BF16), jnp.concatenate(v_parts, axis=0))
    o_win = _normalized(acc_w)
    s_own = s_band[:, WINDOW + tq:WINDOW + 2 * tq]
    m_own = jnp.max(s_own, axis=-1, keepdims=True)
    acc_own = _dot(jnp.exp2(s_own - m_own).astype(BF16), jnp.concatenate([vsl_ref[own, :], ones_q], axis=1))

    def update(kt, s, s_max, carry):
        m, acc = carry
        k0 = pl.multiple_of(kt * tk, tk)
        m_new = jnp.maximum(m, s_max)
        p = jnp.exp2(s - jnp.concatenate([m_new] * (tk // LANES), axis=1))
        alpha = jnp.concatenate([jnp.exp2(m - m_new)] * 2, axis=1)
        v_aug = jnp.concatenate([vsl_ref[pl.ds(k0, tk), :], ones_k], axis=1)
        return m_new, alpha * acc + _dot(p.astype(BF16), v_aug)

    n_kt = jnp.maximum((qs + tk - 1) // tk, 1)

    def body(j, carry):
        kt = 2 * j
        s_cur, s_max = s_ref[0], smax_ref[0]
        put_scores(kt + 1, 1)
        carry = update(kt, s_cur, s_max, carry)
        s_cur, s_max = s_ref[1], smax_ref[1]
        put_scores(jnp.minimum(kt + 2, n_kt - 1), 0)
        return update(kt + 1, s_cur, s_max, carry)

    m_init = jnp.broadcast_to(m_own, (rows, LANES))
    carry = lax.fori_loop(0, n_kt // 2, body, (m_init, acc_own))
    _, acc_s = lax.cond(n_kt % 2 == 1,
                        lambda c: update(n_kt - 1, s_ref[0], smax_ref[0], c),
                        lambda c: c, carry)
    o_slc = _normalized(acc_s)

    o_cmp = ocmp_ref[slot]
    for r in range(GQA_GROUP):
        rs = slice(r * tq, (r + 1) * tq)
        cs = slice(r * HEAD_DIM, (r + 1) * HEAD_DIM)
        y_ref[:, cs] = (gate_ref[r * N_BRANCH] * o_cmp[rs, :] + gate_ref[r * N_BRANCH + 1] * o_slc[rs, :]
                        + gate_ref[r * N_BRANCH + 2] * o_win[rs, :])


def _nsa_attention(q, k_cmp, v_cmp, ksl, vsl, kwn, vwn, cbias, s2c, tri, eye, onehot, wbias, gates):
    B, S, _ = q.shape
    n_blk = k_cmp.shape[2]
    tq = min(TQ, S)
    tk = min(TK_SLC, S)
    n_q = S // tq
    grp = GQA_GROUP * HEAD_DIM
    kernel = functools.partial(_nsa_kernel, tq=tq, tk=tk, n_blk=n_blk)
    kv = pl.BlockSpec((None, S, HEAD_DIM), lambda b, g, i: (b, 0, g))
    cmp_kv = pl.BlockSpec((None, None, n_blk, HEAD_DIM), lambda b, g, i: (b, g, 0, 0))
    const = lambda shape: pl.BlockSpec(shape, lambda b, g, i: (0, 0))
    return pl.pallas_call(
        kernel,
        grid=(B, N_KV_HEADS, S // tq),
        in_specs=[
            pl.BlockSpec((None, tq, grp), lambda b, g, i: (b, i, g)),
            pl.BlockSpec((None, tq, grp), lambda b, g, i: (b, jnp.minimum(i + 1, n_q - 1), g)),
            cmp_kv, cmp_kv, const(cbias.shape), const(s2c.shape), const((LANES, LANES)), const((tq, tq)),
            kv, kv, kv, kv,
            const((S, LANES)), const((WINDOW + tq, tq)),
            pl.BlockSpec((None, tq, LANES), lambda b, g, i: (b, i, 0)),
        ],
        out_specs=pl.BlockSpec((None, tq, grp), lambda b, g, i: (b, i, g)),
        out_shape=jax.ShapeDtypeStruct((B, S, D_ATTN), F32),
        scratch_shapes=[pltpu.VMEM((2, GQA_GROUP * tq, tk), F32),
                        pltpu.VMEM((2, GQA_GROUP * tq, LANES), F32),
                        pltpu.VMEM((2, GQA_GROUP * tq, HEAD_DIM), F32),
                        pltpu.VMEM((2, tq, LANES), BF16),
                        pltpu.VMEM((GQA_GROUP * N_BRANCH, tq, HEAD_DIM), F32)],
        compiler_params=_params(("arbitrary", "arbitrary", "arbitrary")),
        name="nsa_attention",
    )(q, q, k_cmp, v_cmp, cbias, s2c, tri, eye, ksl, vsl, kwn, vwn, onehot, wbias, gates)


def _out_proj_kernel(yc_ref, ya_ref, x_ref, w_ref, ga_ref, g1_ref, n2_ref, sh_ref, sc_ref, x1_ref, h2_ref):
    ya = _rms(ya_ref[...], ga_ref[...]).astype(BF16)
    mix = _dot(yc_ref[...], w_ref[0:D_CONV, :]) + _dot(ya, w_ref[D_CONV:D_CONV + D_ATTN, :])
    x1 = x_ref[...] + g1_ref[...] * mix
    x1_ref[...] = x1
    h2_ref[...] = (_rms(x1, n2_ref[...]) * (1.0 + sc_ref[...]) + sh_ref[...]).astype(BF16)


def _out_proj(yconv, yattn, x, w_out_b, gattn_g, mod, norm2_g):
    B, S, D = x.shape
    tm = min(TM_PROJ, S)
    tok = lambda width: pl.BlockSpec((None, tm, width), lambda b, i: (b, i, 0))
    vec = lambda width: pl.BlockSpec((1, width), lambda b, i: (0, 0))
    modspec = lambda k: pl.BlockSpec((None, None, 1, D), lambda b, i, k=k: (b, k, 0, 0))
    return pl.pallas_call(
        _out_proj_kernel,
        grid=(B, S // tm),
        in_specs=[tok(D_CONV), tok(D_ATTN), tok(D),
                  pl.BlockSpec((D_CONV + D_ATTN, D), lambda b, i: (0, 0), pipeline_mode=pl.Buffered(1)),
                  vec(D_ATTN), modspec(2), vec(D), modspec(3), modspec(4)],
        out_specs=[tok(D), tok(D)],
        out_shape=[jax.ShapeDtypeStruct((B, S, D), F32), jax.ShapeDtypeStruct((B, S, D), BF16)],
        compiler_params=_params(("arbitrary", "arbitrary")),
        name="out_proj",
    )(yconv, yattn, x, w_out_b, gattn_g, mod, norm2_g, mod, mod)


def _ffn_kernel(h_ref, w1_ref, w2_ref, x1_ref, g2_ref, nf_ref, o_ref, *, final_norm):
    f = pl.program_id(2)

    @pl.when(f == 0)
    def _():
        o_ref[...] = jnp.zeros(o_ref.shape, F32)

    a = jnp.maximum(_dot(h_ref[...], w1_ref[...]), 0.0)
    o_ref[...] += _dot((a * a).astype(BF16), w2_ref[...])

    @pl.when(f == pl.num_programs(2) - 1)
    def _():
        x2 = x1_ref[...] + g2_ref[...] * o_ref[...]
        o_ref[...] = _rms(x2, nf_ref[...]) if final_norm else x2


def _ffn(h2, w1_b, w2_b, x1, mod, normf_g, final_norm):
    B, S, D = x1.shape
    d_ff = w1_b.shape[1]
    tm = min(TM_FFN, S)
    tf = min(TF_FFN, d_ff)
    tok = pl.BlockSpec((None, tm, D), lambda b, i, f: (b, i, 0))
    return pl.pallas_call(
        functools.partial(_ffn_kernel, final_norm=final_norm),
        grid=(B, S // tm, d_ff // tf),
        in_specs=[tok,
                  pl.BlockSpec((D, tf), lambda b, i, f: (0, f)),
                  pl.BlockSpec((tf, D), lambda b, i, f: (f, 0)),
                  tok,
                  pl.BlockSpec((None, None, 1, D), lambda b, i, f: (b, 5, 0, 0)),
                  pl.BlockSpec((1, D), lambda b, i, f: (0, 0))],
        out_specs=tok,
        out_shape=jax.ShapeDtypeStruct((B, S, D), F32),
        compiler_params=_params(("arbitrary", "arbitrary", "arbitrary")),
        name="ffn",
    )(h2, w1_b, w2_b, x1, mod, normf_g)


def _rope_tables(seq):
    inv = ROPE_THETA ** (-jnp.arange(0, HEAD_DIM, 2, dtype=F32) / HEAD_DIM)
    ang = jnp.arange(seq, dtype=F32)[:, None] * inv[None, :]
    cos, sin = jnp.cos(ang), jnp.sin(ang)
    return jnp.concatenate([cos, cos], axis=-1), jnp.concatenate([-sin, sin], axis=-1)


def _slc_from_cmp(n_blk):
    sj = np.arange(LANES)[:, None]
    ci = np.arange(n_blk)[None, :]
    m = (ci * CMP_STRIDE <= sj * SLC_LEN + SLC_LEN - 1) & (ci * CMP_STRIDE + CMP_LEN - 1 >= sj * SLC_LEN)
    return jnp.asarray(np.concatenate([m, np.ones((2 * SUBLANES, n_blk), bool)], axis=0), dtype=BF16)


def _cmp_visibility_bias(n_blk, tq):
    d = np.arange(2 * n_blk)[:, None] - n_blk
    r = np.arange(tq)[None, :]
    return jnp.asarray(np.where(CMP_STRIDE * d + CMP_LEN - 1 <= r, 0.0, MASK_BIAS), dtype=F32)


def _block_onehot(seq):
    m = (np.arange(seq)[:, None] // SLC_LEN) == np.arange(LANES)[None, :]
    return jnp.asarray(m, dtype=BF16)


def _prefix_ones():
    return jnp.asarray(np.arange(LANES)[:, None] >= np.arange(LANES)[None, :], dtype=BF16)


def _window_bias(tq):
    key = np.arange(WINDOW + tq)[:, None]
    row = np.arange(tq)[None, :]
    ok = (key > row) & (key <= row + WINDOW)
    return jnp.asarray(np.where(ok, 0.0, MASK_BIAS), dtype=BF16)


def kernel(x, c, w_ada, b_ada, norm1_g, w_in, conv_w, conv_b, cmp_pe_k, cmp_pe_v, cmp_w1_k, cmp_w2_k,
           cmp_w1_v, cmp_w2_v, gnorm_conv_g, gnorm_attn_g, w_out, norm2_g, w_ff1, w_ff2, normf_g):
    B, S, D = x.shape
    depth = w_ada.shape[0]
    assert S % TQ == 0 and S // SLC_LEN <= LANES and S >= WINDOW + TQ
    assert w_in.shape[2] == D_IN
    cos_f, sin_f = _rope_tables(S)
    s2c = _slc_from_cmp(S // CMP_STRIDE)
    cbias = _cmp_visibility_bias(S // CMP_STRIDE, min(TQ, S))
    onehot = _block_onehot(S)
    tri = _prefix_ones()
    eye = jnp.eye(min(TQ, S), dtype=BF16)
    wbias = _window_bias(min(TQ, S))
    for l in range(depth):
        mod = _adaln(c, w_ada[l], b_ada[l]).reshape(B, 6, 1, D)
        w_in_b = jnp.pad(w_in[l].astype(BF16), ((0, 0), (0, D_IN_PAD - D_IN)))
        (yconv, q, kc, vc, ksl, vsl, kwn, vwn, gates) = _in_proj(
            x, mod, norm1_g[l][None], w_in_b, conv_w[l], conv_b[l][None], gnorm_conv_g[l][None], cos_f, sin_f)
        k_cmp = _compress(kc, cmp_pe_k[l], cmp_w1_k[l].astype(BF16), cmp_w2_k[l].astype(BF16))
        v_cmp = _compress(vc, cmp_pe_v[l], cmp_w1_v[l].astype(BF16), cmp_w2_v[l].astype(BF16))
        yattn = _nsa_attention(q, k_cmp, v_cmp, ksl, vsl, kwn, vwn, cbias, s2c, tri, eye, onehot, wbias, gates)
        x1, h2 = _out_proj(yconv, yattn, x, w_out[l].astype(BF16), gnorm_attn_g[l][None], mod, norm2_g[l][None])
        x = _ffn(h2, w_ff1[l].astype(BF16), w_ff2[l].astype(BF16), x1, mod, normf_g[None],
                 final_norm=(l == depth - 1))
    return x
```

```python
import functools

import numpy as np
import jax
import jax.numpy as jnp
from jax import lax
from jax.experimental import pallas as pl
from jax.experimental.pallas import tpu as pltpu

F32 = jnp.float32
BF16 = jnp.bfloat16

HEAD_DIM = 128
N_HEADS = 8
N_KV_HEADS = 2
GQA_GROUP = N_HEADS // N_KV_HEADS
D_CONV = 1024
D_ATTN = N_HEADS * HEAD_DIM
D_KV = N_KV_HEADS * HEAD_DIM
N_BRANCH = 3
CONV_WIDTH = 3
CMP_LEN = 32
CMP_STRIDE = 16
CMP_HIDDEN = 256
SLC_LEN = 64
N_SLC = 16
WINDOW = 512
ROPE_THETA = 10000.0
EPS = 1e-6
FORCE_BONUS = 1e4

LANES = 128
SUBLANES = 8
VMEM_LIMIT_BYTES = 56 * 1024 * 1024
FFN_VMEM_LIMIT_BYTES = 61 * 1024 * 1024

NEG = float(np.finfo(np.float32).min)
MASK_BIAS = -(2.0 ** 126)
LOG2E = float(np.log2(np.e))

COL_UB, COL_UC, COL_UH = 0, D_CONV, 2 * D_CONV
COL_Q = 3 * D_CONV
COL_KV = COL_Q + D_ATTN
COL_GATE = COL_KV + 2 * N_BRANCH * D_KV
D_IN = COL_GATE + N_BRANCH * N_HEADS

TM_PROJ = 512
TQ = 128
TK_SLC = 1024
TM_FFN = 512
TF_FFN = 2048
TN_ADA = 1024


def _params(sem, vmem_limit_bytes=VMEM_LIMIT_BYTES):
    return pltpu.CompilerParams(dimension_semantics=sem, vmem_limit_bytes=vmem_limit_bytes)


def _dot(a, b):
    return jnp.dot(a, b, preferred_element_type=F32)


def _dot_nt(a, b):
    return lax.dot_general(a, b, (((1,), (1,)), ((), ())), preferred_element_type=F32)


def _rms(x, g):
    return x * lax.rsqrt(jnp.mean(x * x, axis=-1, keepdims=True) + EPS) * g


def _adaln_kernel(c_ref, w_ref, b_ref, o_ref):
    c = c_ref[...]
    s = c * jax.nn.sigmoid(c)
    o_ref[...] = _dot(s.astype(BF16), w_ref[...].astype(BF16)) + b_ref[...]


def _adaln(c, w_ada, b_ada):
    B, D = c.shape
    n_out = w_ada.shape[1]
    rows = -(-B // SUBLANES) * SUBLANES
    c_pad = jnp.zeros((rows, D), F32).at[:B].set(c)
    out = pl.pallas_call(
        _adaln_kernel,
        grid=(n_out // TN_ADA,),
        in_specs=[
            pl.BlockSpec((rows, D), lambda j: (0, 0)),
            pl.BlockSpec((D, TN_ADA), lambda j: (0, j)),
            pl.BlockSpec((1, TN_ADA), lambda j: (0, j)),
        ],
        out_specs=pl.BlockSpec((rows, TN_ADA), lambda j: (0, j)),
        out_shape=jax.ShapeDtypeStruct((rows, n_out), F32),
        compiler_params=_params(("arbitrary",)),
        name="adaln",
    )(c_pad, w_ada, b_ada.reshape(1, n_out))
    return out[:B]


def _rope(u, cos_f, sin_f):
    return u * cos_f + pltpu.roll(u, HEAD_DIM // 2, 1) * sin_f


def _in_proj_kernel(x_ref, sh_ref, sc_ref, g_ref, w_ref, wg_ref, cw_ref, cb_ref, gc_ref, cos_ref, sin_ref,
                    yconv_ref, q_ref, kc_ref, vc_ref, ksl_ref, vsl_ref, kwn_ref, vwn_ref, gate_ref,
                    vbuf_ref, *, tm, q_scale):
    @pl.when(pl.program_id(1) == 0)
    def _():
        vbuf_ref[tm:tm + SUBLANES, :] = jnp.zeros((SUBLANES, D_CONV), F32)

    x = x_ref[...]
    h = _rms(x, g_ref[...]) * (1.0 + sc_ref[...]) + sh_ref[...]
    hb = h.astype(BF16)

    ub = _dot(hb, w_ref[:, COL_UB:COL_UB + D_CONV])
    uc = _dot(hb, w_ref[:, COL_UC:COL_UC + D_CONV])
    uh = _dot(hb, w_ref[:, COL_UH:COL_UH + D_CONV])
    v = uc * uh
    vbuf_ref[0:SUBLANES, :] = vbuf_ref[tm:tm + SUBLANES, :]
    vbuf_ref[SUBLANES:SUBLANES + tm, :] = v
    v1 = vbuf_ref[SUBLANES - 1:SUBLANES - 1 + tm, :]
    v2 = vbuf_ref[SUBLANES - 2:SUBLANES - 2 + tm, :]
    z = cb_ref[...] + cw_ref[0:1, :] * v2 + cw_ref[1:2, :] * v1 + cw_ref[2:3, :] * v
    yconv_ref[...] = _rms(ub * z, gc_ref[...]).astype(BF16)

    cos_f = cos_ref[...]
    sin_f = sin_ref[...]

    uq = _dot(hb, w_ref[:, COL_Q:COL_Q + D_ATTN])
    for hd in range(N_HEADS):
        sl = slice(hd * HEAD_DIM, (hd + 1) * HEAD_DIM)
        q_ref[:, sl] = (_rope(uq[:, sl], cos_f, sin_f) * q_scale).astype(BF16)

    ukv = _dot(hb, w_ref[:, COL_KV:COL_GATE])
    outs = (kc_ref, vc_ref, ksl_ref, vsl_ref, kwn_ref, vwn_ref)
    for n, o_ref in enumerate(outs):
        for g in range(N_KV_HEADS):
            src = slice(n * D_KV + g * HEAD_DIM, n * D_KV + (g + 1) * HEAD_DIM)
            dst = slice(g * HEAD_DIM, (g + 1) * HEAD_DIM)
            t = ukv[:, src]
            if n % 2 == 0:
                t = _rope(t, cos_f, sin_f)
            o_ref[:, dst] = t.astype(o_ref.dtype)

    ug = _dot(hb, wg_ref[...])
    gate_ref[...] = jax.nn.sigmoid(ug)


def _in_proj(x, mod, norm1_g, w_in_b, w_gate_b, conv_w, conv_b, gconv_g, cos_f, sin_f):
    B, S, D = x.shape
    tm = min(TM_PROJ, S)
    tok = lambda width: pl.BlockSpec((None, tm, width), lambda b, i: (b, i, 0))
    vec = lambda width: pl.BlockSpec((1, width), lambda b, i: (0, 0))
    modspec = lambda k: pl.BlockSpec((None, None, 1, D), lambda b, i, k=k: (b, k, 0, 0))
    kernel = functools.partial(_in_proj_kernel, tm=tm, q_scale=LOG2E * HEAD_DIM ** -0.5)
    sd = jax.ShapeDtypeStruct
    return pl.pallas_call(
        kernel,
        grid=(B, S // tm),
        in_specs=[
            tok(D), modspec(0), modspec(1), vec(D),
            pl.BlockSpec((D, D_IN), lambda b, i: (0, 0), pipeline_mode=pl.Buffered(1)),
            pl.BlockSpec((D, LANES), lambda b, i: (0, 0)),
            pl.BlockSpec((CONV_WIDTH, D_CONV), lambda b, i: (0, 0)), vec(D_CONV), vec(D_CONV),
            pl.BlockSpec((tm, HEAD_DIM), lambda b, i: (i, 0)),
            pl.BlockSpec((tm, HEAD_DIM), lambda b, i: (i, 0)),
        ],
        out_specs=[tok(D_CONV), tok(D_ATTN), tok(D_KV), tok(D_KV), tok(D_KV), tok(D_KV), tok(D_KV),
                   tok(D_KV), tok(LANES)],
        out_shape=[sd((B, S, D_CONV), BF16), sd((B, S, D_ATTN), BF16),
                   sd((B, S, D_KV), F32), sd((B, S, D_KV), F32),
                   sd((B, S, D_KV), BF16), sd((B, S, D_KV), BF16),
                   sd((B, S, D_KV), BF16), sd((B, S, D_KV), BF16),
                   sd((B, S, LANES), F32)],
        scratch_shapes=[pltpu.VMEM((tm + 2 * SUBLANES, D_CONV), F32)],
        compiler_params=_params(("arbitrary", "arbitrary")),
        name="in_proj",
    )(x, mod, mod, norm1_g, w_in_b, w_gate_b, conv_w, conv_b, gconv_g, cos_f, sin_f)


def _compress_kernel(kv_ref, pe_ref, w1_ref, w2_ref, o_ref, buf_ref, *, seq, n_blk):
    buf_ref[0:seq, :] = kv_ref[...]
    buf_ref[seq:seq + CMP_STRIDE, :] = jnp.zeros((CMP_STRIDE, HEAD_DIM), F32)
    def token(l):
        return (buf_ref[pl.ds(l, n_blk, stride=CMP_STRIDE), :] + pe_ref[l:l + 1, :]).astype(BF16)

    acc = jnp.zeros((n_blk, CMP_HIDDEN), F32)
    for l in range(0, CMP_LEN, 2):
        pair = jnp.concatenate([token(l), token(l + 1)], axis=1)
        acc = acc + _dot(pair, w1_ref[l * HEAD_DIM:(l + 2) * HEAD_DIM, :])
    hid = jax.nn.gelu(acc)
    o_ref[...] = _dot(hid.astype(BF16), w2_ref[...]).astype(BF16)


def _compress(kv, pe, w1_b, w2_b):
    B, S, _ = kv.shape
    n_blk = S // CMP_STRIDE
    kernel = functools.partial(_compress_kernel, seq=S, n_blk=n_blk)
    return pl.pallas_call(
        kernel,
        grid=(B, N_KV_HEADS),
        in_specs=[
            pl.BlockSpec((None, S, HEAD_DIM), lambda b, g: (b, 0, g)),
            pl.BlockSpec((CMP_LEN, HEAD_DIM), lambda b, g: (0, 0)),
            pl.BlockSpec((CMP_LEN * HEAD_DIM, CMP_HIDDEN), lambda b, g: (0, 0)),
            pl.BlockSpec((CMP_HIDDEN, HEAD_DIM), lambda b, g: (0, 0)),
        ],
        out_specs=pl.BlockSpec((None, None, n_blk, HEAD_DIM), lambda b, g: (b, g, 0, 0)),
        out_shape=jax.ShapeDtypeStruct((B, N_KV_HEADS, n_blk, HEAD_DIM), BF16),
        scratch_shapes=[pltpu.VMEM((S + CMP_STRIDE, HEAD_DIM), F32)],
        compiler_params=_params(("arbitrary", "arbitrary")),
        name="compress",
    )(kv, pe, w1_b, w2_b)


def _stack_heads(q_ref):
    return jnp.concatenate([q_ref[:, r * HEAD_DIM:(r + 1) * HEAD_DIM] for r in range(GQA_GROUP)], axis=0)


def _normalized(acc):
    l = acc[:, HEAD_DIM:2 * HEAD_DIM]
    return acc[:, 0:HEAD_DIM] / jnp.where(l > 0.0, l, 1.0)


def _gate_col(gates, g, r, branch):
    lo = r * N_BRANCH + branch
    hi = (GQA_GROUP + r) * N_BRANCH + branch
    return jnp.where(g == 0, gates[:, lo:lo + 1], gates[:, hi:hi + 1])


def _cmp_scores(q_band, kc_ref, cbias_ref, qs, *, n_blk):
    start = pl.multiple_of(n_blk - qs // CMP_STRIDE, SUBLANES)
    vis_bias = cbias_ref[pl.ds(start, n_blk), :].astype(BF16)
    return _dot_nt(q_band, jnp.concatenate([kc_ref[...], vis_bias], axis=1))


def _cmp_attend(s, vc_ref, s2c_ref, qs, *, tq, n_blk):
    rows = GQA_GROUP * tq
    p = jnp.exp2(s - jnp.max(s, axis=-1, keepdims=True)).astype(BF16)
    ov = _dot(p, jnp.concatenate([vc_ref[...], jnp.ones((n_blk, HEAD_DIM), BF16)], axis=1))
    t_row = qs + (lax.broadcasted_iota(jnp.int32, (rows, HEAD_DIM), 0) & (tq - 1))
    o_cmp = jnp.where(t_row >= CMP_LEN - 1, _normalized(ov), 0.0)

    imp = jnp.zeros((LANES, tq), F32)
    for r in range(GQA_GROUP):
        it = _dot_nt(s2c_ref[...], p[r * tq:(r + 1) * tq, :])
        l_t = it[LANES:LANES + 1, :]
        imp = imp + it[0:LANES, :] / jnp.where(l_t > 0.0, l_t, 1.0)
    t = qs + lax.broadcasted_iota(jnp.int32, (LANES, tq), 1)
    imp = jnp.where(t >= CMP_LEN - 1, imp, 0.0)
    blk = lax.broadcasted_iota(jnp.int32, (LANES, tq), 0)
    cur = t // SLC_LEN
    valid = blk * SLC_LEN <= t
    forced = (blk == 0) | (blk == cur) | (blk == cur - 1)
    return o_cmp, jnp.where(valid, imp + jnp.where(forced, FORCE_BONUS, 0.0), -1.0), valid


def _cmp_select(score, valid, tri_ref, eye_ref, qs, *, tq):
    x = score
    covered = jnp.zeros((1, tq), F32)
    n_before = jnp.zeros((1, tq), F32)
    theta = jnp.full((1, tq), -1.0, F32)
    for _ in range(N_SLC):
        mx = jnp.max(x, axis=0, keepdims=True)
        eq = x == mx
        cnt = jnp.sum(jnp.where(eq, 1.0, 0.0), axis=0, keepdims=True)
        crossing = (covered < float(N_SLC)) & (covered + cnt >= float(N_SLC))
        theta = jnp.where(crossing, mx, theta)
        n_before = jnp.where(crossing, covered, n_before)
        covered = covered + cnt
        x = jnp.where(eq, -jnp.inf, x)
    tie = score == theta
    tie_rank = _dot(tri_ref[...], jnp.where(tie, 1.0, 0.0).astype(BF16))
    chosen = (score > theta) | (tie & (tie_rank <= float(N_SLC) - n_before))
    sel_t = jnp.where(valid & chosen, 1.0, 0.0).astype(BF16)
    sel = _dot_nt(eye_ref[...], sel_t)
    below = lax.broadcasted_iota(jnp.int32, (tq, LANES), 1) < qs // SLC_LEN
    return jnp.where(below, (sel - 1.0) * (-MASK_BIAS), MASK_BIAS).astype(BF16)


def _nsa_kernel(q_ref, qn_ref, kc_ref, vc_ref, cbias_ref, s2c_ref, tri_ref, eye_ref, ksl_ref, vsl_ref, kwn_ref,
                vwn_ref, onehot_ref, wbias_ref, gates_in_ref, y_ref, s_ref, smax_ref, ocmp_ref, selb_ref, gate_ref,
                *, tq, tk, n_blk):
    g = pl.program_id(1)
    i = pl.program_id(2)
    qs = i * tq
    slot = i & 1
    rows = GQA_GROUP * tq
    row_id = lax.broadcasted_iota(jnp.int32, (rows, tq), 0) & (tq - 1)
    eye = jnp.where(row_id == lax.broadcasted_iota(jnp.int32, (rows, tq), 1), 1.0, 0.0).astype(BF16)

    @pl.when(i == 0)
    def _():
        q0 = jnp.concatenate([_stack_heads(q_ref), eye], axis=1)
        s0 = _cmp_scores(q0, kc_ref, cbias_ref, qs, n_blk=n_blk)
        ocmp_ref[0], score0, valid0 = _cmp_attend(s0, vc_ref, s2c_ref, qs, tq=tq, n_blk=n_blk)
        selb_ref[0] = _cmp_select(score0, valid0, tri_ref, eye_ref, qs, tq=tq)

    qst = _stack_heads(q_ref)
    ones_k = jnp.ones((tk, HEAD_DIM), BF16)
    ones_q = jnp.ones((tq, HEAD_DIM), BF16)
    q_band = jnp.concatenate([qst, eye], axis=1)
    sel_bias = selb_ref[slot]
    gates = gates_in_ref[...]
    for r in range(GQA_GROUP):
        for br in range(N_BRANCH):
            gate_ref[r * N_BRANCH + br] = jnp.broadcast_to(_gate_col(gates, g, r, br), (tq, HEAD_DIM))

    qs_next = qs + tq
    q_next = jnp.concatenate([_stack_heads(qn_ref), eye], axis=1)
    s_next = _cmp_scores(q_next, kc_ref, cbias_ref, qs_next, n_blk=n_blk)

    n_chunk = WINDOW // tq + 1
    k_parts, v_parts = [], []
    for c in range(n_chunk):
        start = qs - WINDOW + c * tq
        src = pl.multiple_of(jnp.maximum(start, 0), tq)
        bias_c = wbias_ref[c * tq:(c + 1) * tq, :]
        bias_c = jnp.where(start >= 0, bias_c, jnp.full(bias_c.shape, MASK_BIAS, BF16))
        k_parts.append(jnp.concatenate([kwn_ref[pl.ds(src, tq), :], bias_c], axis=1))
        v_parts.append(jnp.concatenate([vwn_ref[pl.ds(src, tq), :], ones_q], axis=1))
    own = pl.ds(pl.multiple_of(qs, tq), tq)
    causal_bias = wbias_ref[(n_chunk - 1) * tq:n_chunk * tq, :]
    k_parts.append(jnp.concatenate([ksl_ref[own, :], causal_bias], axis=1))
    s_band = _dot_nt(q_band, jnp.concatenate(k_parts, axis=0))

    ocmp_ref[1 - slot], score_next, valid_next = _cmp_attend(s_next, vc_ref, s2c_ref, qs_next, tq=tq, n_blk=n_blk)

    q_slc = jnp.concatenate([qst, jnp.concatenate([sel_bias] * GQA_GROUP, axis=0)], axis=1)

    def put_scores(kt, slot):
        k0 = pl.multiple_of(kt * tk, tk)
        k_aug = jnp.concatenate([ksl_ref[pl.ds(k0, tk), :], onehot_ref[pl.ds(k0, tk), :]], axis=1)
        s = _dot_nt(q_slc, k_aug)
        s_ref[slot] = s
        smax_ref[slot] = jnp.broadcast_to(jnp.max(s, axis=-1, keepdims=True), (rows, LANES))

    put_scores(0, 0)

    selb_ref[1 - slot] = _cmp_select(score_next, valid_next, tri_ref, eye_ref, qs_next, tq=tq)

    s_w = s_band[:, 0:WINDOW + tq]
    p_w = jnp.exp2(s_w - jnp.max(s_w, axis=-1, keepdims=True))
    acc_w = _dot(p_w.astype(BF16), jnp.concatenate(v_parts, axis=0))
    o_win = _normalized(acc_w)
    s_own = s_band[:, WINDOW + tq:WINDOW + 2 * tq]
    m_own = jnp.max(s_own, axis=-1, keepdims=True)
    acc_own = _dot(jnp.exp2(s_own - m_own).astype(BF16), jnp.concatenate([vsl_ref[own, :], ones_q], axis=1))

    def update(kt, s, s_max, carry):
        m, acc = carry
        k0 = pl.multiple_of(kt * tk, tk)
        m_new = jnp.maximum(m, s_max)
        p = jnp.exp2(s - jnp.concatenate([m_new] * (tk // LANES), axis=1))
        alpha = jnp.concatenate([jnp.exp2(m - m_new)] * 2, axis=1)
        v_aug = jnp.concatenate([vsl_ref[pl.ds(k0, tk), :], ones_k], axis=1)
        return m_new, alpha * acc + _dot(p.astype(BF16), v_aug)

    n_kt = jnp.maximum((qs + tk - 1) // tk, 1)

    def body(j, carry):
        kt = 2 * j
        s_cur, s_max = s_ref[0], smax_ref[0]
        put_scores(kt + 1, 1)
        carry = update(kt, s_cur, s_max, carry)
        s_cur, s_max = s_ref[1], smax_ref[1]
        put_scores(jnp.minimum(kt + 2, n_kt - 1), 0)
        return update(kt + 1, s_cur, s_max, carry)

    m_init = jnp.broadcast_to(m_own, (rows, LANES))
    carry = lax.fori_loop(0, n_kt // 2, body, (m_init, acc_own))
    _, acc_s = lax.cond(n_kt % 2 == 1,
                        lambda c: update(n_kt - 1, s_ref[0], smax_ref[0], c),
                        lambda c: c, carry)
    o_slc = _normalized(acc_s)

    o_cmp = ocmp_ref[slot]
    for r in range(GQA_GROUP):
        rs = slice(r * tq, (r + 1) * tq)
        cs = slice(r * HEAD_DIM, (r + 1) * HEAD_DIM)
        y_ref[:, cs] = (gate_ref[r * N_BRANCH] * o_cmp[rs, :] + gate_ref[r * N_BRANCH + 1] * o_slc[rs, :]
                        + gate_ref[r * N_BRANCH + 2] * o_win[rs, :])


def _nsa_attention(q, k_cmp, v_cmp, ksl, vsl, kwn, vwn, cbias, s2c, tri, eye, onehot, wbias, gates):
    B, S, _ = q.shape
    n_blk = k_cmp.shape[2]
    tq = min(TQ, S)
    tk = min(TK_SLC, S)
    n_q = S // tq
    grp = GQA_GROUP * HEAD_DIM
    kernel = functools.partial(_nsa_kernel, tq=tq, tk=tk, n_blk=n_blk)
    kv = pl.BlockSpec((None, S, HEAD_DIM), lambda b, g, i: (b, 0, g))
    cmp_kv = pl.BlockSpec((None, None, n_blk, HEAD_DIM), lambda b, g, i: (b, g, 0, 0))
    const = lambda shape: pl.BlockSpec(shape, lambda b, g, i: (0, 0))
    return pl.pallas_call(
        kernel,
        grid=(B, N_KV_HEADS, S // tq),
        in_specs=[
            pl.BlockSpec((None, tq, grp), lambda b, g, i: (b, i, g)),
            pl.BlockSpec((None, tq, grp), lambda b, g, i: (b, jnp.minimum(i + 1, n_q - 1), g)),
            cmp_kv, cmp_kv, const(cbias.shape), const(s2c.shape), const((LANES, LANES)), const((tq, tq)),
            kv, kv, kv, kv,
            const((S, LANES)), const((WINDOW + tq, tq)),
            pl.BlockSpec((None, tq, LANES), lambda b, g, i: (b, i, 0)),
        ],
        out_specs=pl.BlockSpec((None, tq, grp), lambda b, g, i: (b, i, g)),
        out_shape=jax.ShapeDtypeStruct((B, S, D_ATTN), F32),
        scratch_shapes=[pltpu.VMEM((2, GQA_GROUP * tq, tk), F32),
                        pltpu.VMEM((2, GQA_GROUP * tq, LANES), F32),
                        pltpu.VMEM((2, GQA_GROUP * tq, HEAD_DIM), F32),
                        pltpu.VMEM((2, tq, LANES), BF16),
                        pltpu.VMEM((GQA_GROUP * N_BRANCH, tq, HEAD_DIM), F32)],
        compiler_params=_params(("arbitrary", "arbitrary", "arbitrary")),
        name="nsa_attention",
    )(q, q, k_cmp, v_cmp, cbias, s2c, tri, eye, ksl, vsl, kwn, vwn, onehot, wbias, gates)


def _out_proj_kernel(yc_ref, ya_ref, x_ref, w_ref, ga_ref, g1_ref, n2_ref, sh_ref, sc_ref, x1_ref, h2_ref):
    ya = _rms(ya_ref[...], ga_ref[...]).astype(BF16)
    mix = _dot(yc_ref[...], w_ref[0:D_CONV, :]) + _dot(ya, w_ref[D_CONV:D_CONV + D_ATTN, :])
    x1 = x_ref[...] + g1_ref[...] * mix
    x1_ref[...] = x1
    h2_ref[...] = (_rms(x1, n2_ref[...]) * (1.0 + sc_ref[...]) + sh_ref[...]).astype(BF16)


def _out_proj(yconv, yattn, x, w_out_b, gattn_g, mod, norm2_g):
    B, S, D = x.shape
    tm = min(TM_PROJ, S)
    tok = lambda width: pl.BlockSpec((None, tm, width), lambda b, i: (b, i, 0))
    vec = lambda width: pl.BlockSpec((1, width), lambda b, i: (0, 0))
    modspec = lambda k: pl.BlockSpec((None, None, 1, D), lambda b, i, k=k: (b, k, 0, 0))
    return pl.pallas_call(
        _out_proj_kernel,
        grid=(B, S // tm),
        in_specs=[tok(D_CONV), tok(D_ATTN), tok(D),
                  pl.BlockSpec((D_CONV + D_ATTN, D), lambda b, i: (0, 0), pipeline_mode=pl.Buffered(1)),
                  vec(D_ATTN), modspec(2), vec(D), modspec(3), modspec(4)],
        out_specs=[tok(D), tok(D)],
        out_shape=[jax.ShapeDtypeStruct((B, S, D), F32), jax.ShapeDtypeStruct((B, S, D), BF16)],
        compiler_params=_params(("arbitrary", "arbitrary")),
        name="out_proj",
    )(yconv, yattn, x, w_out_b, gattn_g, mod, norm2_g, mod, mod)


def _ffn_kernel(h_ref, w1_ref, w2_ref, x1_ref, g2_ref, nf_ref, o_ref, *, final_norm):
    f = pl.program_id(2)

    @pl.when(f == 0)
    def _():
        o_ref[...] = jnp.zeros(o_ref.shape, F32)

    a = jnp.maximum(_dot(h_ref[...], w1_ref[...]), 0.0)
    o_ref[...] += _dot((a * a).astype(BF16), w2_ref[...])

    @pl.when(f == pl.num_programs(2) - 1)
    def _():
        x2 = x1_ref[...] + g2_ref[...] * o_ref[...]
        o_ref[...] = _rms(x2, nf_ref[...]) if final_norm else x2


def _ffn(h2, w1_b, w2_b, x1, mod, normf_g, final_norm):
    B, S, D = x1.shape
    d_ff = w1_b.shape[1]
    tm = min(TM_FFN, S)
    tf = min(TF_FFN, d_ff)
    tok = pl.BlockSpec((None, tm, D), lambda b, i, f: (b, i, 0))
    return pl.pallas_call(
        functools.partial(_ffn_kernel, final_norm=final_norm),
        grid=(B, S // tm, d_ff // tf),
        in_specs=[tok,
                  pl.BlockSpec((D, tf), lambda b, i, f: (0, f)),
                  pl.BlockSpec((tf, D), lambda b, i, f: (f, 0)),
                  tok,
                  pl.BlockSpec((None, None, 1, D), lambda b, i, f: (b, 5, 0, 0)),
                  pl.BlockSpec((1, D), lambda b, i, f: (0, 0))],
        out_specs=tok,
        out_shape=jax.ShapeDtypeStruct((B, S, D), F32),
        compiler_params=_params(("arbitrary", "arbitrary", "arbitrary"), FFN_VMEM_LIMIT_BYTES),
        name="ffn",
    )(h2, w1_b, w2_b, x1, mod, normf_g)


def _rope_tables(seq):
    inv = ROPE_THETA ** (-jnp.arange(0, HEAD_DIM, 2, dtype=F32) / HEAD_DIM)
    ang = jnp.arange(seq, dtype=F32)[:, None] * inv[None, :]
    cos, sin = jnp.cos(ang), jnp.sin(ang)
    return jnp.concatenate([cos, cos], axis=-1), jnp.concatenate([-sin, sin], axis=-1)


def _slc_from_cmp(n_blk):
    sj = np.arange(LANES)[:, None]
    ci = np.arange(n_blk)[None, :]
    m = (ci * CMP_STRIDE <= sj * SLC_LEN + SLC_LEN - 1) & (ci * CMP_STRIDE + CMP_LEN - 1 >= sj * SLC_LEN)
    return jnp.asarray(np.concatenate([m, np.ones((2 * SUBLANES, n_blk), bool)], axis=0), dtype=BF16)


def _cmp_visibility_bias(n_blk, tq):
    d = np.arange(2 * n_blk)[:, None] - n_blk
    r = np.arange(tq)[None, :]
    return jnp.asarray(np.where(CMP_STRIDE * d + CMP_LEN - 1 <= r, 0.0, MASK_BIAS), dtype=F32)


def _block_onehot(seq):
    m = (np.arange(seq)[:, None] // SLC_LEN) == np.arange(LANES)[None, :]
    return jnp.asarray(m, dtype=BF16)


def _prefix_ones():
    return jnp.asarray(np.arange(LANES)[:, None] >= np.arange(LANES)[None, :], dtype=BF16)


def _window_bias(tq):
    key = np.arange(WINDOW + tq)[:, None]
    row = np.arange(tq)[None, :]
    ok = (key > row) & (key <= row + WINDOW)
    return jnp.asarray(np.where(ok, 0.0, MASK_BIAS), dtype=BF16)


def kernel(x, c, w_ada, b_ada, norm1_g, w_in, conv_w, conv_b, cmp_pe_k, cmp_pe_v, cmp_w1_k, cmp_w2_k,
           cmp_w1_v, cmp_w2_v, gnorm_conv_g, gnorm_attn_g, w_out, norm2_g, w_ff1, w_ff2, normf_g):
    B, S, D = x.shape
    depth = w_ada.shape[0]
    assert S % TQ == 0 and S // SLC_LEN <= LANES and S >= WINDOW + TQ
    assert w_in.shape[2] == D_IN
    cos_f, sin_f = _rope_tables(S)
    s2c = _slc_from_cmp(S // CMP_STRIDE)
    cbias = _cmp_visibility_bias(S // CMP_STRIDE, min(TQ, S))
    onehot = _block_onehot(S)
    tri = _prefix_ones()
    eye = jnp.eye(min(TQ, S), dtype=BF16)
    wbias = _window_bias(min(TQ, S))
    for l in range(depth):
        mod = _adaln(c, w_ada[l], b_ada[l]).reshape(B, 6, 1, D)
        w_in_b = w_in[l].astype(BF16)
        w_gate_b = jnp.pad(w_in[l][:, COL_GATE:], ((0, 0), (0, LANES - N_BRANCH * N_HEADS))).astype(BF16)
        (yconv, q, kc, vc, ksl, vsl, kwn, vwn, gates) = _in_proj(
            x, mod, norm1_g[l][None], w_in_b, w_gate_b, conv_w[l], conv_b[l][None], gnorm_conv_g[l][None], cos_f, sin_f)
        k_cmp = _compress(kc, cmp_pe_k[l], cmp_w1_k[l].astype(BF16), cmp_w2_k[l].astype(BF16))
        v_cmp = _compress(vc, cmp_pe_v[l], cmp_w1_v[l].astype(BF16), cmp_w2_v[l].astype(BF16))
        yattn = _nsa_attention(q, k_cmp, v_cmp, ksl, vsl, kwn, vwn, cbias, s2c, tri, eye, onehot, wbias, gates)
        x1, h2 = _out_proj(yconv, yattn, x, w_out[l].astype(BF16), gnorm_attn_g[l][None], mod, norm2_g[l][None])
        x = _ffn(h2, w_ff1[l].astype(BF16), w_ff2[l].astype(BF16), x1, mod, normf_g[None],
                 final_norm=(l == depth - 1))
    return x
```

```python
import functools

import numpy as np
import jax
import jax.numpy as jnp
from jax import lax
from jax.experimental import pallas as pl
from jax.experimental.pallas import tpu as pltpu

F32 = jnp.float32
BF16 = jnp.bfloat16

HEAD_DIM = 128
N_HEADS = 8
N_KV_HEADS = 2
GQA_GROUP = N_HEADS // N_KV_HEADS
D_CONV = 1024
D_ATTN = N_HEADS * HEAD_DIM
D_KV = N_KV_HEADS * HEAD_DIM
N_BRANCH = 3
CONV_WIDTH = 3
CMP_LEN = 32
CMP_STRIDE = 16
CMP_HIDDEN = 256
SLC_LEN = 64
N_SLC = 16
WINDOW = 512
ROPE_THETA = 10000.0
EPS = 1e-6
FORCE_BONUS = 1e4

LANES = 128
SUBLANES = 8
VMEM_LIMIT_BYTES = 56 * 1024 * 1024
FFN_VMEM_LIMIT_BYTES = 61 * 1024 * 1024

NEG = float(np.finfo(np.float32).min)
MASK_BIAS = -(2.0 ** 126)
LOG2E = float(np.log2(np.e))

COL_UB, COL_UC, COL_UH = 0, D_CONV, 2 * D_CONV
COL_Q = 3 * D_CONV
COL_KV = COL_Q + D_ATTN
COL_GATE = COL_KV + 2 * N_BRANCH * D_KV
D_IN = COL_GATE + N_BRANCH * N_HEADS

TM_PROJ = 512
TQ = 128
TK_SLC = 1024
TM_FFN = 512
TF_FFN = 2048
TN_ADA = 1024


def _params(sem, vmem_limit_bytes=VMEM_LIMIT_BYTES):
    return pltpu.CompilerParams(dimension_semantics=sem, vmem_limit_bytes=vmem_limit_bytes)


def _dot(a, b):
    return jnp.dot(a, b, preferred_element_type=F32)


def _dot_nt(a, b):
    return lax.dot_general(a, b, (((1,), (1,)), ((), ())), preferred_element_type=F32)


def _rms(x, g):
    return x * lax.rsqrt(jnp.mean(x * x, axis=-1, keepdims=True) + EPS) * g


def _adaln_kernel(c_ref, w_ref, b_ref, o_ref):
    c = c_ref[...]
    s = c * jax.nn.sigmoid(c)
    o_ref[...] = _dot(s.astype(BF16), w_ref[...].astype(BF16)) + b_ref[...]


def _adaln(c, w_ada, b_ada):
    B, D = c.shape
    n_out = w_ada.shape[1]
    rows = -(-B // SUBLANES) * SUBLANES
    c_pad = jnp.zeros((rows, D), F32).at[:B].set(c)
    out = pl.pallas_call(
        _adaln_kernel,
        grid=(n_out // TN_ADA,),
        in_specs=[
            pl.BlockSpec((rows, D), lambda j: (0, 0)),
            pl.BlockSpec((D, TN_ADA), lambda j: (0, j)),
            pl.BlockSpec((1, TN_ADA), lambda j: (0, j)),
        ],
        out_specs=pl.BlockSpec((rows, TN_ADA), lambda j: (0, j)),
        out_shape=jax.ShapeDtypeStruct((rows, n_out), F32),
        compiler_params=_params(("arbitrary",)),
        name="adaln",
    )(c_pad, w_ada, b_ada.reshape(1, n_out))
    return out[:B]


def _rope(u, cos_f, sin_f):
    return u * cos_f + pltpu.roll(u, HEAD_DIM // 2, 1) * sin_f


def _in_proj_kernel(x_ref, sh_ref, sc_ref, g_ref, w_ref, wg_ref, cw_ref, cb_ref, gc_ref, cos_ref, sin_ref,
                    yconv_ref, q_ref, kc_ref, vc_ref, ksl_ref, vsl_ref, kwn_ref, vwn_ref, gate_ref,
                    vbuf_ref, *, tm, q_scale):
    @pl.when(pl.program_id(1) == 0)
    def _():
        vbuf_ref[tm:tm + SUBLANES, :] = jnp.zeros((SUBLANES, D_CONV), F32)

    x = x_ref[...]
    h = _rms(x, g_ref[...]) * (1.0 + sc_ref[...]) + sh_ref[...]
    hb = h.astype(BF16)

    ub = _dot(hb, w_ref[:, COL_UB:COL_UB + D_CONV])
    uc = _dot(hb, w_ref[:, COL_UC:COL_UC + D_CONV])
    uh = _dot(hb, w_ref[:, COL_UH:COL_UH + D_CONV])
    v = uc * uh
    vbuf_ref[0:SUBLANES, :] = vbuf_ref[tm:tm + SUBLANES, :]
    vbuf_ref[SUBLANES:SUBLANES + tm, :] = v
    v1 = vbuf_ref[SUBLANES - 1:SUBLANES - 1 + tm, :]
    v2 = vbuf_ref[SUBLANES - 2:SUBLANES - 2 + tm, :]
    z = cb_ref[...] + cw_ref[0:1, :] * v2 + cw_ref[1:2, :] * v1 + cw_ref[2:3, :] * v
    yconv_ref[...] = _rms(ub * z, gc_ref[...]).astype(BF16)

    cos_f = cos_ref[...]
    sin_f = sin_ref[...]

    uq = _dot(hb, w_ref[:, COL_Q:COL_Q + D_ATTN])
    for hd in range(N_HEADS):
        sl = slice(hd * HEAD_DIM, (hd + 1) * HEAD_DIM)
        q_ref[:, sl] = (_rope(uq[:, sl], cos_f, sin_f) * q_scale).astype(BF16)

    ukv = _dot(hb, w_ref[:, COL_KV:COL_GATE])
    outs = (kc_ref, vc_ref, ksl_ref, vsl_ref, kwn_ref, vwn_ref)
    for n, o_ref in enumerate(outs):
        for g in range(N_KV_HEADS):
            src = slice(n * D_KV + g * HEAD_DIM, n * D_KV + (g + 1) * HEAD_DIM)
            dst = slice(g * HEAD_DIM, (g + 1) * HEAD_DIM)
            t = ukv[:, src]
            if n % 2 == 0:
                t = _rope(t, cos_f, sin_f)
            o_ref[:, dst] = t.astype(o_ref.dtype)

    ug = _dot(hb, wg_ref[...])
    gate_ref[...] = jax.nn.sigmoid(ug)


def _in_proj(x, mod, norm1_g, w_in_b, w_gate_b, conv_w, conv_b, gconv_g, cos_f, sin_f):
    B, S, D = x.shape
    tm = min(TM_PROJ, S)
    tok = lambda width: pl.BlockSpec((None, tm, width), lambda b, i: (b, i, 0))
    vec = lambda width: pl.BlockSpec((1, width), lambda b, i: (0, 0))
    modspec = lambda k: pl.BlockSpec((None, None, 1, D), lambda b, i, k=k: (b, k, 0, 0))
    kernel = functools.partial(_in_proj_kernel, tm=tm, q_scale=LOG2E * HEAD_DIM ** -0.5)
    sd = jax.ShapeDtypeStruct
    return pl.pallas_call(
        kernel,
        grid=(B, S // tm),
        in_specs=[
            tok(D), modspec(0), modspec(1), vec(D),
            pl.BlockSpec((D, D_IN), lambda b, i: (0, 0), pipeline_mode=pl.Buffered(1)),
            pl.BlockSpec((D, LANES), lambda b, i: (0, 0)),
            pl.BlockSpec((CONV_WIDTH, D_CONV), lambda b, i: (0, 0)), vec(D_CONV), vec(D_CONV),
            pl.BlockSpec((tm, HEAD_DIM), lambda b, i: (i, 0)),
            pl.BlockSpec((tm, HEAD_DIM), lambda b, i: (i, 0)),
        ],
        out_specs=[tok(D_CONV), tok(D_ATTN), tok(D_KV), tok(D_KV), tok(D_KV), tok(D_KV), tok(D_KV),
                   tok(D_KV), tok(LANES)],
        out_shape=[sd((B, S, D_CONV), BF16), sd((B, S, D_ATTN), BF16),
                   sd((B, S, D_KV), F32), sd((B, S, D_KV), F32),
                   sd((B, S, D_KV), BF16), sd((B, S, D_KV), BF16),
                   sd((B, S, D_KV), BF16), sd((B, S, D_KV), BF16),
                   sd((B, S, LANES), F32)],
        scratch_shapes=[pltpu.VMEM((tm + 2 * SUBLANES, D_CONV), F32)],
        compiler_params=_params(("arbitrary", "arbitrary")),
        name="in_proj",
    )(x, mod, mod, norm1_g, w_in_b, w_gate_b, conv_w, conv_b, gconv_g, cos_f, sin_f)


def _compress_kernel(kv_ref, pe_ref, w1_ref, w2_ref, o_ref, buf_ref, *, seq, n_blk):
    buf_ref[0:seq, :] = kv_ref[...]
    buf_ref[seq:seq + CMP_STRIDE, :] = jnp.zeros((CMP_STRIDE, HEAD_DIM), F32)
    def token(l):
        return (buf_ref[pl.ds(l, n_blk, stride=CMP_STRIDE), :] + pe_ref[l:l + 1, :]).astype(BF16)

    acc = jnp.zeros((n_blk, CMP_HIDDEN), F32)
    for l in range(0, CMP_LEN, 2):
        pair = jnp.concatenate([token(l), token(l + 1)], axis=1)
        acc = acc + _dot(pair, w1_ref[l * HEAD_DIM:(l + 2) * HEAD_DIM, :])
    hid = jax.nn.gelu(acc)
    o_ref[...] = _dot(hid.astype(BF16), w2_ref[...]).astype(BF16)


def _compress(kv, pe, w1_b, w2_b):
    B, S, _ = kv.shape
    n_blk = S // CMP_STRIDE
    kernel = functools.partial(_compress_kernel, seq=S, n_blk=n_blk)
    return pl.pallas_call(
        kernel,
        grid=(B, N_KV_HEADS),
        in_specs=[
            pl.BlockSpec((None, S, HEAD_DIM), lambda b, g: (b, 0, g)),
            pl.BlockSpec((CMP_LEN, HEAD_DIM), lambda b, g: (0, 0)),
            pl.BlockSpec((CMP_LEN * HEAD_DIM, CMP_HIDDEN), lambda b, g: (0, 0)),
            pl.BlockSpec((CMP_HIDDEN, HEAD_DIM), lambda b, g: (0, 0)),
        ],
        out_specs=pl.BlockSpec((None, None, n_blk, HEAD_DIM), lambda b, g: (b, g, 0, 0)),
        out_shape=jax.ShapeDtypeStruct((B, N_KV_HEADS, n_blk, HEAD_DIM), BF16),
        scratch_shapes=[pltpu.VMEM((S + CMP_STRIDE, HEAD_DIM), F32)],
        compiler_params=_params(("arbitrary", "arbitrary")),
        name="compress",
    )(kv, pe, w1_b, w2_b)


def _stack_heads(q_ref):
    return jnp.concatenate([q_ref[:, r * HEAD_DIM:(r + 1) * HEAD_DIM] for r in range(GQA_GROUP)], axis=0)


def _normalized(acc):
    l = acc[:, HEAD_DIM:2 * HEAD_DIM]
    return acc[:, 0:HEAD_DIM] / jnp.where(l > 0.0, l, 1.0)


def _gate_col(gates, g, r, branch):
    lo = r * N_BRANCH + branch
    hi = (GQA_GROUP + r) * N_BRANCH + branch
    return jnp.where(g == 0, gates[:, lo:lo + 1], gates[:, hi:hi + 1])


def _cmp_scores(q_band, kc_ref, cbias_ref, qs, *, n_blk):
    start = pl.multiple_of(n_blk - qs // CMP_STRIDE, SUBLANES)
    vis_bias = cbias_ref[pl.ds(start, n_blk), :].astype(BF16)
    return _dot_nt(q_band, jnp.concatenate([kc_ref[...], vis_bias], axis=1))


def _cmp_attend(s, vc_ref, s2c_ref, qs, *, tq, n_blk):
    rows = GQA_GROUP * tq
    p = jnp.exp2(s - jnp.max(s, axis=-1, keepdims=True)).astype(BF16)
    ov = _dot(p, jnp.concatenate([vc_ref[...], jnp.ones((n_blk, HEAD_DIM), BF16)], axis=1))
    t_row = qs + (lax.broadcasted_iota(jnp.int32, (rows, HEAD_DIM), 0) & (tq - 1))
    o_cmp = jnp.where(t_row >= CMP_LEN - 1, _normalized(ov), 0.0)

    imp = jnp.zeros((LANES, tq), F32)
    for r in range(GQA_GROUP):
        it = _dot_nt(s2c_ref[...], p[r * tq:(r + 1) * tq, :])
        l_t = it[LANES:LANES + 1, :]
        imp = imp + it[0:LANES, :] / jnp.where(l_t > 0.0, l_t, 1.0)
    t = qs + lax.broadcasted_iota(jnp.int32, (LANES, tq), 1)
    imp = jnp.where(t >= CMP_LEN - 1, imp, 0.0)
    blk = lax.broadcasted_iota(jnp.int32, (LANES, tq), 0)
    cur = t // SLC_LEN
    valid = blk * SLC_LEN <= t
    forced = (blk == 0) | (blk == cur) | (blk == cur - 1)
    return o_cmp, jnp.where(valid, imp + jnp.where(forced, FORCE_BONUS, 0.0), -1.0), valid


def _cmp_select(score, valid, tri_ref, eye_ref, qs, *, tq):
    x = score
    covered = jnp.zeros((1, tq), F32)
    n_before = jnp.zeros((1, tq), F32)
    theta = jnp.full((1, tq), -1.0, F32)
    for _ in range(N_SLC):
        mx = jnp.max(x, axis=0, keepdims=True)
        eq = x == mx
        cnt = jnp.sum(jnp.where(eq, 1.0, 0.0), axis=0, keepdims=True)
        crossing = (covered < float(N_SLC)) & (covered + cnt >= float(N_SLC))
        theta = jnp.where(crossing, mx, theta)
        n_before = jnp.where(crossing, covered, n_before)
        covered = covered + cnt
        x = jnp.where(eq, -jnp.inf, x)
    tie = score == theta
    tie_rank = _dot(tri_ref[...], jnp.where(tie, 1.0, 0.0).astype(BF16))
    chosen = (score > theta) | (tie & (tie_rank <= float(N_SLC) - n_before))
    sel_t = jnp.where(valid & chosen, 1.0, 0.0).astype(BF16)
    sel = _dot_nt(eye_ref[...], sel_t)
    below = lax.broadcasted_iota(jnp.int32, (tq, LANES), 1) < qs // SLC_LEN
    return jnp.where(below, (sel - 1.0) * (-MASK_BIAS), MASK_BIAS).astype(BF16)


def _nsa_kernel(q_ref, qn_ref, kc_ref, vc_ref, cbias_ref, s2c_ref, tri_ref, eye_ref, ksl_ref, vsl_ref, kwn_ref,
                vwn_ref, onehot_ref, wbias_ref, gates_in_ref, y_ref, s_ref, smax_ref, ocmp_ref, selb_ref, gate_ref,
                *, tq, tk, n_blk):
    g = pl.program_id(1)
    i = pl.program_id(2)
    qs = i * tq
    slot = i & 1
    rows = GQA_GROUP * tq
    row_id = lax.broadcasted_iota(jnp.int32, (rows, tq), 0) & (tq - 1)
    eye = jnp.where(row_id == lax.broadcasted_iota(jnp.int32, (rows, tq), 1), 1.0, 0.0).astype(BF16)

    @pl.when(i == 0)
    def _():
        q0 = jnp.concatenate([_stack_heads(q_ref), eye], axis=1)
        s0 = _cmp_scores(q0, kc_ref, cbias_ref, qs, n_blk=n_blk)
        ocmp_ref[0], score0, valid0 = _cmp_attend(s0, vc_ref, s2c_ref, qs, tq=tq, n_blk=n_blk)
        selb_ref[0] = _cmp_select(score0, valid0, tri_ref, eye_ref, qs, tq=tq)

    qst = _stack_heads(q_ref)
    ones_k = jnp.ones((tk, HEAD_DIM), BF16)
    ones_q = jnp.ones((tq, HEAD_DIM), BF16)
    q_band = jnp.concatenate([qst, eye], axis=1)
    sel_bias = selb_ref[slot]
    gates = gates_in_ref[...]
    for r in range(GQA_GROUP):
        for br in range(N_BRANCH):
            gate_ref[r * N_BRANCH + br] = jnp.broadcast_to(_gate_col(gates, g, r, br), (tq, HEAD_DIM))

    qs_next = qs + tq
    q_next = jnp.concatenate([_stack_heads(qn_ref), eye], axis=1)
    s_next = _cmp_scores(q_next, kc_ref, cbias_ref, qs_next, n_blk=n_blk)

    n_chunk = WINDOW // tq + 1
    k_parts, v_parts = [], []
    for c in range(n_chunk):
        start = qs - WINDOW + c * tq
        src = pl.multiple_of(jnp.maximum(start, 0), tq)
        bias_c = wbias_ref[c * tq:(c + 1) * tq, :]
        bias_c = jnp.where(start >= 0, bias_c, jnp.full(bias_c.shape, MASK_BIAS, BF16))
        k_parts.append(jnp.concatenate([kwn_ref[pl.ds(src, tq), :], bias_c], axis=1))
        v_parts.append(jnp.concatenate([vwn_ref[pl.ds(src, tq), :], ones_q], axis=1))
    own = pl.ds(pl.multiple_of(qs, tq), tq)
    causal_bias = wbias_ref[(n_chunk - 1) * tq:n_chunk * tq, :]
    k_parts.append(jnp.concatenate([ksl_ref[own, :], causal_bias], axis=1))
    s_band = _dot_nt(q_band, jnp.concatenate(k_parts, axis=0))

    ocmp_ref[1 - slot], score_next, valid_next = _cmp_attend(s_next, vc_ref, s2c_ref, qs_next, tq=tq, n_blk=n_blk)

    q_slc = jnp.concatenate([qst, jnp.concatenate([sel_bias] * GQA_GROUP, axis=0)], axis=1)

    def put_scores(kt, slot):
        k0 = pl.multiple_of(kt * tk, tk)
        k_aug = jnp.concatenate([ksl_ref[pl.ds(k0, tk), :], onehot_ref[pl.ds(k0, tk), :]], axis=1)
        s = _dot_nt(q_slc, k_aug)
        s_ref[slot] = s
        smax_ref[slot] = jnp.broadcast_to(jnp.max(s, axis=-1, keepdims=True), (rows, LANES))

    put_scores(0, 0)

    s_w = s_band[:, 0:WINDOW + tq]
    p_w = jnp.exp2(s_w - jnp.max(s_w, axis=-1, keepdims=True))
    acc_w = _dot(p_w.astype(BF16), jnp.concatenate(v_parts, axis=0))
    o_win = _normalized(acc_w)
    s_own = s_band[:, WINDOW + tq:WINDOW + 2 * tq]
    m_own = jnp.max(s_own, axis=-1, keepdims=True)
    acc_own = _dot(jnp.exp2(s_own - m_own).astype(BF16), jnp.concatenate([vsl_ref[own, :], ones_q], axis=1))

    selb_ref[1 - slot] = _cmp_select(score_next, valid_next, tri_ref, eye_ref, qs_next, tq=tq)

    def update(kt, s, s_max, carry):
        m, acc = carry
        k0 = pl.multiple_of(kt * tk, tk)
        m_new = jnp.maximum(m, s_max)
        p = jnp.exp2(s - jnp.concatenate([m_new] * (tk // LANES), axis=1))
        alpha = jnp.concatenate([jnp.exp2(m - m_new)] * 2, axis=1)
        v_aug = jnp.concatenate([vsl_ref[pl.ds(k0, tk), :], ones_k], axis=1)
        return m_new, alpha * acc + _dot(p.astype(BF16), v_aug)

    n_kt = jnp.maximum((qs + tk - 1) // tk, 1)

    def body(j, carry):
        kt = 2 * j
        s_cur, s_max = s_ref[0], smax_ref[0]
        put_scores(kt + 1, 1)
        carry = update(kt, s_cur, s_max, carry)
        s_cur, s_max = s_ref[1], smax_ref[1]
        put_scores(jnp.minimum(kt + 2, n_kt - 1), 0)
        return update(kt + 1, s_cur, s_max, carry)

    m_init = jnp.broadcast_to(m_own, (rows, LANES))
    carry = lax.fori_loop(0, n_kt // 2, body, (m_init, acc_own))
    _, acc_s = lax.cond(n_kt % 2 == 1,
                        lambda c: update(n_kt - 1, s_ref[0], smax_ref[0], c),
                        lambda c: c, carry)
    o_slc = _normalized(acc_s)

    o_cmp = ocmp_ref[slot]
    for r in range(GQA_GROUP):
        rs = slice(r * tq, (r + 1) * tq)
        cs = slice(r * HEAD_DIM, (r + 1) * HEAD_DIM)
        y_ref[:, cs] = (gate_ref[r * N_BRANCH] * o_cmp[rs, :] + gate_ref[r * N_BRANCH + 1] * o_slc[rs, :]
                        + gate_ref[r * N_BRANCH + 2] * o_win[rs, :])


def _nsa_attention(q, k_cmp, v_cmp, ksl, vsl, kwn, vwn, cbias, s2c, tri, eye, onehot, wbias, gates):
    B, S, _ = q.shape
    n_blk = k_cmp.shape[2]
    tq = min(TQ, S)
    tk = min(TK_SLC, S)
    n_q = S // tq
    grp = GQA_GROUP * HEAD_DIM
    kernel = functools.partial(_nsa_kernel, tq=tq, tk=tk, n_blk=n_blk)
    kv = pl.BlockSpec((None, S, HEAD_DIM), lambda b, g, i: (b, 0, g))
    cmp_kv = pl.BlockSpec((None, None, n_blk, HEAD_DIM), lambda b, g, i: (b, g, 0, 0))
    const = lambda shape: pl.BlockSpec(shape, lambda b, g, i: (0, 0))
    return pl.pallas_call(
        kernel,
        grid=(B, N_KV_HEADS, S // tq),
        in_specs=[
            pl.BlockSpec((None, tq, grp), lambda b, g, i: (b, i, g)),
            pl.BlockSpec((None, tq, grp), lambda b, g, i: (b, jnp.minimum(i + 1, n_q - 1), g)),
            cmp_kv, cmp_kv, const(cbias.shape), const(s2c.shape), const((LANES, LANES)), const((tq, tq)),
            kv, kv, kv, kv,
            const((S, LANES)), const((WINDOW + tq, tq)),
            pl.BlockSpec((None, tq, LANES), lambda b, g, i: (b, i, 0)),
        ],
        out_specs=pl.BlockSpec((None, tq, grp), lambda b, g, i: (b, i, g)),
        out_shape=jax.ShapeDtypeStruct((B, S, D_ATTN), F32),
        scratch_shapes=[pltpu.VMEM((2, GQA_GROUP * tq, tk), F32),
                        pltpu.VMEM((2, GQA_GROUP * tq, LANES), F32),
                        pltpu.VMEM((2, GQA_GROUP * tq, HEAD_DIM), F32),
                        pltpu.VMEM((2, tq, LANES), BF16),
                        pltpu.VMEM((GQA_GROUP * N_BRANCH, tq, HEAD_DIM), F32)],
        compiler_params=_params(("arbitrary", "arbitrary", "arbitrary")),
        name="nsa_attention",
    )(q, q, k_cmp, v_cmp, cbias, s2c, tri, eye, ksl, vsl, kwn, vwn, onehot, wbias, gates)


def _out_proj_kernel(yc_ref, ya_ref, x_ref, w_ref, ga_ref, g1_ref, n2_ref, sh_ref, sc_ref, x1_ref, h2_ref):
    ya = _rms(ya_ref[...], ga_ref[...]).astype(BF16)
    mix = _dot(yc_ref[...], w_ref[0:D_CONV, :]) + _dot(ya, w_ref[D_CONV:D_CONV + D_ATTN, :])
    x1 = x_ref[...] + g1_ref[...] * mix
    x1_ref[...] = x1
    h2_ref[...] = (_rms(x1, n2_ref[...]) * (1.0 + sc_ref[...]) + sh_ref[...]).astype(BF16)


def _out_proj(yconv, yattn, x, w_out_b, gattn_g, mod, norm2_g):
    B, S, D = x.shape
    tm = min(TM_PROJ, S)
    tok = lambda width: pl.BlockSpec((None, tm, width), lambda b, i: (b, i, 0))
    vec = lambda width: pl.BlockSpec((1, width), lambda b, i: (0, 0))
    modspec = lambda k: pl.BlockSpec((None, None, 1, D), lambda b, i, k=k: (b, k, 0, 0))
    return pl.pallas_call(
        _out_proj_kernel,
        grid=(B, S // tm),
        in_specs=[tok(D_CONV), tok(D_ATTN), tok(D),
                  pl.BlockSpec((D_CONV + D_ATTN, D), lambda b, i: (0, 0), pipeline_mode=pl.Buffered(1)),
                  vec(D_ATTN), modspec(2), vec(D), modspec(3), modspec(4)],
        out_specs=[tok(D), tok(D)],
        out_shape=[jax.ShapeDtypeStruct((B, S, D), F32), jax.ShapeDtypeStruct((B, S, D), BF16)],
        compiler_params=_params(("arbitrary", "arbitrary")),
        name="out_proj",
    )(yconv, yattn, x, w_out_b, gattn_g, mod, norm2_g, mod, mod)


def _ffn_kernel(h_ref, w1_ref, w2_ref, x1_ref, g2_ref, nf_ref, o_ref, *, final_norm):
    f = pl.program_id(2)

    @pl.when(f == 0)
    def _():
        o_ref[...] = jnp.zeros(o_ref.shape, F32)

    a = jnp.maximum(_dot(h_ref[...], w1_ref[...]), 0.0)
    o_ref[...] += _dot((a * a).astype(BF16), w2_ref[...])

    @pl.when(f == pl.num_programs(2) - 1)
    def _():
        x2 = x1_ref[...] + g2_ref[...] * o_ref[...]
        o_ref[...] = _rms(x2, nf_ref[...]) if final_norm else x2


def _ffn(h2, w1_b, w2_b, x1, mod, normf_g, final_norm):
    B, S, D = x1.shape
    d_ff = w1_b.shape[1]
    tm = min(TM_FFN, S)
    tf = min(TF_FFN, d_ff)
    tok = pl.BlockSpec((None, tm, D), lambda b, i, f: (b, i, 0))
    return pl.pallas_call(
        functools.partial(_ffn_kernel, final_norm=final_norm),
        grid=(B, S // tm, d_ff // tf),
        in_specs=[tok,
                  pl.BlockSpec((D, tf), lambda b, i, f: (0, f)),
                  pl.BlockSpec((tf, D), lambda b, i, f: (f, 0)),
                  tok,
                  pl.BlockSpec((None, None, 1, D), lambda b, i, f: (b, 5, 0, 0)),
                  pl.BlockSpec((1, D), lambda b, i, f: (0, 0))],
        out_specs=tok,
        out_shape=jax.ShapeDtypeStruct((B, S, D), F32),
        compiler_params=_params(("arbitrary", "arbitrary", "arbitrary"), FFN_VMEM_LIMIT_BYTES),
        name="ffn",
    )(h2, w1_b, w2_b, x1, mod, normf_g)


def _rope_tables(seq):
    inv = ROPE_THETA ** (-jnp.arange(0, HEAD_DIM, 2, dtype=F32) / HEAD_DIM)
    ang = jnp.arange(seq, dtype=F32)[:, None] * inv[None, :]
    cos, sin = jnp.cos(ang), jnp.sin(ang)
    return jnp.concatenate([cos, cos], axis=-1), jnp.concatenate([-sin, sin], axis=-1)


def _slc_from_cmp(n_blk):
    sj = np.arange(LANES)[:, None]
    ci = np.arange(n_blk)[None, :]
    m = (ci * CMP_STRIDE <= sj * SLC_LEN + SLC_LEN - 1) & (ci * CMP_STRIDE + CMP_LEN - 1 >= sj * SLC_LEN)
    return jnp.asarray(np.concatenate([m, np.ones((2 * SUBLANES, n_blk), bool)], axis=0), dtype=BF16)


def _cmp_visibility_bias(n_blk, tq):
    d = np.arange(2 * n_blk)[:, None] - n_blk
    r = np.arange(tq)[None, :]
    return jnp.asarray(np.where(CMP_STRIDE * d + CMP_LEN - 1 <= r, 0.0, MASK_BIAS), dtype=F32)


def _block_onehot(seq):
    m = (np.arange(seq)[:, None] // SLC_LEN) == np.arange(LANES)[None, :]
    return jnp.asarray(m, dtype=BF16)


def _prefix_ones():
    return jnp.asarray(np.arange(LANES)[:, None] >= np.arange(LANES)[None, :], dtype=BF16)


def _window_bias(tq):
    key = np.arange(WINDOW + tq)[:, None]
    row = np.arange(tq)[None, :]
    ok = (key > row) & (key <= row + WINDOW)
    return jnp.asarray(np.where(ok, 0.0, MASK_BIAS), dtype=BF16)


def kernel(x, c, w_ada, b_ada, norm1_g, w_in, conv_w, conv_b, cmp_pe_k, cmp_pe_v, cmp_w1_k, cmp_w2_k,
           cmp_w1_v, cmp_w2_v, gnorm_conv_g, gnorm_attn_g, w_out, norm2_g, w_ff1, w_ff2, normf_g):
    B, S, D = x.shape
    depth = w_ada.shape[0]
    assert S % TQ == 0 and S // SLC_LEN <= LANES and S >= WINDOW + TQ
    assert w_in.shape[2] == D_IN
    cos_f, sin_f = _rope_tables(S)
    s2c = _slc_from_cmp(S // CMP_STRIDE)
    cbias = _cmp_visibility_bias(S // CMP_STRIDE, min(TQ, S))
    onehot = _block_onehot(S)
    tri = _prefix_ones()
    eye = jnp.eye(min(TQ, S), dtype=BF16)
    wbias = _window_bias(min(TQ, S))
    for l in range(depth):
        mod = _adaln(c, w_ada[l], b_ada[l]).reshape(B, 6, 1, D)
        w_in_b = w_in[l].astype(BF16)
        w_gate_b = jnp.pad(w_in[l][:, COL_GATE:], ((0, 0), (0, LANES - N_BRANCH * N_HEADS))).astype(BF16)
        (yconv, q, kc, vc, ksl, vsl, kwn, vwn, gates) = _in_proj(
            x, mod, norm1_g[l][None], w_in_b, w_gate_b, conv_w[l], conv_b[l][None], gnorm_conv_g[l][None], cos_f, sin_f)
        k_cmp = _compress(kc, cmp_pe_k[l], cmp_w1_k[l].astype(BF16), cmp_w2_k[l].astype(BF16))
        v_cmp = _compress(vc, cmp_pe_v[l], cmp_w1_v[l].astype(BF16), cmp_w2_v[l].astype(BF16))
        yattn = _nsa_attention(q, k_cmp, v_cmp, ksl, vsl, kwn, vwn, cbias, s2c, tri, eye, onehot, wbias, gates)
        x1, h2 = _out_proj(yconv, yattn, x, w_out[l].astype(BF16), gnorm_attn_g[l][None], mod, norm2_g[l][None])
        x = _ffn(h2, w_ff1[l].astype(BF16), w_ff2[l].astype(BF16), x1, mod, normf_g[None],
                 final_norm=(l == depth - 1))
    return x
```

```python
import functools

import numpy as np
import jax
import jax.numpy as jnp
from jax import lax
from jax.experimental import pallas as pl
from jax.experimental.pallas import tpu as pltpu

F32 = jnp.float32
BF16 = jnp.bfloat16

HEAD_DIM = 128
N_HEADS = 8
N_KV_HEADS = 2
GQA_GROUP = N_HEADS // N_KV_HEADS
D_CONV = 1024
D_ATTN = N_HEADS * HEAD_DIM
D_KV = N_KV_HEADS * HEAD_DIM
N_BRANCH = 3
CONV_WIDTH = 3
CMP_LEN = 32
CMP_STRIDE = 16
CMP_HIDDEN = 256
SLC_LEN = 64
N_SLC = 16
WINDOW = 512
ROPE_THETA = 10000.0
EPS = 1e-6
FORCE_BONUS = 1e4

LANES = 128
SUBLANES = 8
VMEM_LIMIT_BYTES = 56 * 1024 * 1024
FFN_VMEM_LIMIT_BYTES = 61 * 1024 * 1024

NEG = float(np.finfo(np.float32).min)
MASK_BIAS = -(2.0 ** 126)
LOG2E = float(np.log2(np.e))

COL_UB, COL_UC, COL_UH = 0, D_CONV, 2 * D_CONV
COL_Q = 3 * D_CONV
COL_KV = COL_Q + D_ATTN
COL_GATE = COL_KV + 2 * N_BRANCH * D_KV
D_IN = COL_GATE + N_BRANCH * N_HEADS

TM_PROJ = 512
TQ = 128
TK_SLC = 1024
TM_FFN = 512
TF_FFN = 2048
TN_ADA = 1024


def _params(sem, vmem_limit_bytes=VMEM_LIMIT_BYTES):
    return pltpu.CompilerParams(dimension_semantics=sem, vmem_limit_bytes=vmem_limit_bytes)


def _dot(a, b):
    return jnp.dot(a, b, preferred_element_type=F32)


def _dot_nt(a, b):
    return lax.dot_general(a, b, (((1,), (1,)), ((), ())), preferred_element_type=F32)


def _rms(x, g):
    return x * lax.rsqrt(jnp.mean(x * x, axis=-1, keepdims=True) + EPS) * g


def _adaln_kernel(c_ref, w_ref, b_ref, o_ref):
    c = c_ref[...]
    s = c * jax.nn.sigmoid(c)
    o_ref[...] = _dot(s.astype(BF16), w_ref[...].astype(BF16)) + b_ref[...]


def _adaln(c, w_ada, b_ada):
    B, D = c.shape
    n_out = w_ada.shape[1]
    rows = -(-B // SUBLANES) * SUBLANES
    c_pad = jnp.zeros((rows, D), F32).at[:B].set(c)
    out = pl.pallas_call(
        _adaln_kernel,
        grid=(n_out // TN_ADA,),
        in_specs=[
            pl.BlockSpec((rows, D), lambda j: (0, 0)),
            pl.BlockSpec((D, TN_ADA), lambda j: (0, j)),
            pl.BlockSpec((1, TN_ADA), lambda j: (0, j)),
        ],
        out_specs=pl.BlockSpec((rows, TN_ADA), lambda j: (0, j)),
        out_shape=jax.ShapeDtypeStruct((rows, n_out), F32),
        compiler_params=_params(("arbitrary",)),
        name="adaln",
    )(c_pad, w_ada, b_ada.reshape(1, n_out))
    return out[:B]


def _rope(u, cos_f, sin_f):
    return u * cos_f + pltpu.roll(u, HEAD_DIM // 2, 1) * sin_f


def _in_proj_kernel(x_ref, sh_ref, sc_ref, g_ref, w_ref, wg_ref, cw_ref, cb_ref, gc_ref, cos_ref, sin_ref,
                    yconv_ref, q_ref, kc_ref, vc_ref, ksl_ref, vsl_ref, kwn_ref, vwn_ref, gate_ref,
                    vbuf_ref, *, tm, q_scale):
    @pl.when(pl.program_id(1) == 0)
    def _():
        vbuf_ref[tm:tm + SUBLANES, :] = jnp.zeros((SUBLANES, D_CONV), F32)

    x = x_ref[...]
    h = _rms(x, g_ref[...]) * (1.0 + sc_ref[...]) + sh_ref[...]
    hb = h.astype(BF16)

    ub = _dot(hb, w_ref[:, COL_UB:COL_UB + D_CONV])
    uc = _dot(hb, w_ref[:, COL_UC:COL_UC + D_CONV])
    uh = _dot(hb, w_ref[:, COL_UH:COL_UH + D_CONV])
    v = uc * uh
    vbuf_ref[0:SUBLANES, :] = vbuf_ref[tm:tm + SUBLANES, :]
    vbuf_ref[SUBLANES:SUBLANES + tm, :] = v
    v1 = vbuf_ref[SUBLANES - 1:SUBLANES - 1 + tm, :]
    v2 = vbuf_ref[SUBLANES - 2:SUBLANES - 2 + tm, :]
    z = cb_ref[...] + cw_ref[0:1, :] * v2 + cw_ref[1:2, :] * v1 + cw_ref[2:3, :] * v
    yconv_ref[...] = _rms(ub * z, gc_ref[...]).astype(BF16)

    cos_f = cos_ref[...]
    sin_f = sin_ref[...]

    uq = _dot(hb, w_ref[:, COL_Q:COL_Q + D_ATTN])
    for hd in range(N_HEADS):
        sl = slice(hd * HEAD_DIM, (hd + 1) * HEAD_DIM)
        q_ref[:, sl] = (_rope(uq[:, sl], cos_f, sin_f) * q_scale).astype(BF16)

    ug = _dot(hb, wg_ref[...])
    gate_ref[...] = jax.nn.sigmoid(ug)

    outs = (kc_ref, vc_ref, ksl_ref, vsl_ref, kwn_ref, vwn_ref)
    for n in (0, 2, 4, 1, 3, 5):
        u = _dot(hb, w_ref[:, COL_KV + n * D_KV:COL_KV + (n + 1) * D_KV])
        for g in range(N_KV_HEADS):
            cs = slice(g * HEAD_DIM, (g + 1) * HEAD_DIM)
            t = _rope(u[:, cs], cos_f, sin_f) if n % 2 == 0 else u[:, cs]
            outs[n][:, cs] = t.astype(outs[n].dtype)


def _in_proj(x, mod, norm1_g, w_in_b, w_gate_b, conv_w, conv_b, gconv_g, cos_f, sin_f):
    B, S, D = x.shape
    tm = min(TM_PROJ, S)
    tok = lambda width: pl.BlockSpec((None, tm, width), lambda b, i: (b, i, 0))
    vec = lambda width: pl.BlockSpec((1, width), lambda b, i: (0, 0))
    modspec = lambda k: pl.BlockSpec((None, None, 1, D), lambda b, i, k=k: (b, k, 0, 0))
    kernel = functools.partial(_in_proj_kernel, tm=tm, q_scale=LOG2E * HEAD_DIM ** -0.5)
    sd = jax.ShapeDtypeStruct
    return pl.pallas_call(
        kernel,
        grid=(B, S // tm),
        in_specs=[
            tok(D), modspec(0), modspec(1), vec(D),
            pl.BlockSpec((D, D_IN), lambda b, i: (0, 0), pipeline_mode=pl.Buffered(1)),
            pl.BlockSpec((D, LANES), lambda b, i: (0, 0)),
            pl.BlockSpec((CONV_WIDTH, D_CONV), lambda b, i: (0, 0)), vec(D_CONV), vec(D_CONV),
            pl.BlockSpec((tm, HEAD_DIM), lambda b, i: (i, 0)),
            pl.BlockSpec((tm, HEAD_DIM), lambda b, i: (i, 0)),
        ],
        out_specs=[tok(D_CONV), tok(D_ATTN), tok(D_KV), tok(D_KV), tok(D_KV), tok(D_KV), tok(D_KV),
                   tok(D_KV), tok(LANES)],
        out_shape=[sd((B, S, D_CONV), BF16), sd((B, S, D_ATTN), BF16),
                   sd((B, S, D_KV), F32), sd((B, S, D_KV), F32),
                   sd((B, S, D_KV), BF16), sd((B, S, D_KV), BF16),
                   sd((B, S, D_KV), BF16), sd((B, S, D_KV), BF16),
                   sd((B, S, LANES), F32)],
        scratch_shapes=[pltpu.VMEM((tm + 2 * SUBLANES, D_CONV), F32)],
        compiler_params=_params(("arbitrary", "arbitrary")),
        name="in_proj",
    )(x, mod, mod, norm1_g, w_in_b, w_gate_b, conv_w, conv_b, gconv_g, cos_f, sin_f)


def _compress_kernel(kv_ref, pe_ref, w1_ref, w2_ref, o_ref, buf_ref, *, seq, n_blk):
    buf_ref[0:seq, :] = kv_ref[...]
    buf_ref[seq:seq + CMP_STRIDE, :] = jnp.zeros((CMP_STRIDE, HEAD_DIM), F32)
    def token(l):
        return (buf_ref[pl.ds(l, n_blk, stride=CMP_STRIDE), :] + pe_ref[l:l + 1, :]).astype(BF16)

    acc = jnp.zeros((n_blk, CMP_HIDDEN), F32)
    for l in range(0, CMP_LEN, 2):
        pair = jnp.concatenate([token(l), token(l + 1)], axis=1)
        acc = acc + _dot(pair, w1_ref[l * HEAD_DIM:(l + 2) * HEAD_DIM, :])
    hid = jax.nn.gelu(acc)
    o_ref[...] = _dot(hid.astype(BF16), w2_ref[...]).astype(BF16)


def _compress(kv, pe, w1_b, w2_b):
    B, S, _ = kv.shape
    n_blk = S // CMP_STRIDE
    kernel = functools.partial(_compress_kernel, seq=S, n_blk=n_blk)
    return pl.pallas_call(
        kernel,
        grid=(B, N_KV_HEADS),
        in_specs=[
            pl.BlockSpec((None, S, HEAD_DIM), lambda b, g: (b, 0, g)),
            pl.BlockSpec((CMP_LEN, HEAD_DIM), lambda b, g: (0, 0)),
            pl.BlockSpec((CMP_LEN * HEAD_DIM, CMP_HIDDEN), lambda b, g: (0, 0)),
            pl.BlockSpec((CMP_HIDDEN, HEAD_DIM), lambda b, g: (0, 0)),
        ],
        out_specs=pl.BlockSpec((None, None, n_blk, HEAD_DIM), lambda b, g: (b, g, 0, 0)),
        out_shape=jax.ShapeDtypeStruct((B, N_KV_HEADS, n_blk, HEAD_DIM), BF16),
        scratch_shapes=[pltpu.VMEM((S + CMP_STRIDE, HEAD_DIM), F32)],
        compiler_params=_params(("arbitrary", "arbitrary")),
        name="compress",
    )(kv, pe, w1_b, w2_b)


def _stack_heads(q_ref):
    return jnp.concatenate([q_ref[:, r * HEAD_DIM:(r + 1) * HEAD_DIM] for r in range(GQA_GROUP)], axis=0)


def _normalized(acc):
    l = acc[:, HEAD_DIM:2 * HEAD_DIM]
    return acc[:, 0:HEAD_DIM] / jnp.where(l > 0.0, l, 1.0)


def _gate_col(gates, g, r, branch):
    lo = r * N_BRANCH + branch
    hi = (GQA_GROUP + r) * N_BRANCH + branch
    return jnp.where(g == 0, gates[:, lo:lo + 1], gates[:, hi:hi + 1])


def _cmp_scores(q_band, kc_ref, cbias_ref, qs, *, n_blk):
    start = pl.multiple_of(n_blk - qs // CMP_STRIDE, SUBLANES)
    vis_bias = cbias_ref[pl.ds(start, n_blk), :].astype(BF16)
    return _dot_nt(q_band, jnp.concatenate([kc_ref[...], vis_bias], axis=1))


def _cmp_attend(s, vc_ref, s2c_ref, qs, *, tq, n_blk):
    rows = GQA_GROUP * tq
    p = jnp.exp2(s - jnp.max(s, axis=-1, keepdims=True)).astype(BF16)
    ov = _dot(p, jnp.concatenate([vc_ref[...], jnp.ones((n_blk, HEAD_DIM), BF16)], axis=1))
    t_row = qs + (lax.broadcasted_iota(jnp.int32, (rows, HEAD_DIM), 0) & (tq - 1))
    o_cmp = jnp.where(t_row >= CMP_LEN - 1, _normalized(ov), 0.0)

    imp = jnp.zeros((LANES, tq), F32)
    for r in range(GQA_GROUP):
        it = _dot_nt(s2c_ref[...], p[r * tq:(r + 1) * tq, :])
        l_t = it[LANES:LANES + 1, :]
        imp = imp + it[0:LANES, :] / jnp.where(l_t > 0.0, l_t, 1.0)
    t = qs + lax.broadcasted_iota(jnp.int32, (LANES, tq), 1)
    imp = jnp.where(t >= CMP_LEN - 1, imp, 0.0)
    blk = lax.broadcasted_iota(jnp.int32, (LANES, tq), 0)
    cur = t // SLC_LEN
    valid = blk * SLC_LEN <= t
    forced = (blk == 0) | (blk == cur) | (blk == cur - 1)
    return o_cmp, jnp.where(valid, imp + jnp.where(forced, FORCE_BONUS, 0.0), -1.0), valid


def _cmp_select(score, valid, tri_ref, eye_ref, qs, *, tq):
    x = score
    covered = jnp.zeros((1, tq), F32)
    n_before = jnp.zeros((1, tq), F32)
    theta = jnp.full((1, tq), -1.0, F32)
    for _ in range(N_SLC):
        mx = jnp.max(x, axis=0, keepdims=True)
        eq = x == mx
        cnt = jnp.sum(jnp.where(eq, 1.0, 0.0), axis=0, keepdims=True)
        crossing = (covered < float(N_SLC)) & (covered + cnt >= float(N_SLC))
        theta = jnp.where(crossing, mx, theta)
        n_before = jnp.where(crossing, covered, n_before)
        covered = covered + cnt
        x = jnp.where(eq, -jnp.inf, x)
    tie = score == theta
    tie_rank = _dot(tri_ref[...], jnp.where(tie, 1.0, 0.0).astype(BF16))
    chosen = (score > theta) | (tie & (tie_rank <= float(N_SLC) - n_before))
    sel_t = jnp.where(valid & chosen, 1.0, 0.0).astype(BF16)
    sel = _dot_nt(eye_ref[...], sel_t)
    below = lax.broadcasted_iota(jnp.int32, (tq, LANES), 1) < qs // SLC_LEN
    return jnp.where(below, (sel - 1.0) * (-MASK_BIAS), MASK_BIAS).astype(BF16)


def _nsa_kernel(q_ref, qn_ref, kc_ref, vc_ref, cbias_ref, s2c_ref, tri_ref, eye_ref, ksl_ref, vsl_ref, kwn_ref,
                vwn_ref, onehot_ref, wbias_ref, gates_in_ref, y_ref, s_ref, smax_ref, ocmp_ref, selb_ref, gate_ref,
                *, tq, tk, n_blk):
    g = pl.program_id(1)
    i = pl.program_id(2)
    qs = i * tq
    slot = i & 1
    rows = GQA_GROUP * tq
    row_id = lax.broadcasted_iota(jnp.int32, (rows, tq), 0) & (tq - 1)
    eye = jnp.where(row_id == lax.broadcasted_iota(jnp.int32, (rows, tq), 1), 1.0, 0.0).astype(BF16)

    @pl.when(i == 0)
    def _():
        q0 = jnp.concatenate([_stack_heads(q_ref), eye], axis=1)
        s0 = _cmp_scores(q0, kc_ref, cbias_ref, qs, n_blk=n_blk)
        ocmp_ref[0], score0, valid0 = _cmp_attend(s0, vc_ref, s2c_ref, qs, tq=tq, n_blk=n_blk)
        selb_ref[0] = _cmp_select(score0, valid0, tri_ref, eye_ref, qs, tq=tq)

    qst = _stack_heads(q_ref)
    ones_k = jnp.ones((tk, HEAD_DIM), BF16)
    ones_q = jnp.ones((tq, HEAD_DIM), BF16)
    q_band = jnp.concatenate([qst, eye], axis=1)
    sel_bias = selb_ref[slot]
    gates = gates_in_ref[...]
    for r in range(GQA_GROUP):
        for br in range(N_BRANCH):
            gate_ref[r * N_BRANCH + br] = jnp.broadcast_to(_gate_col(gates, g, r, br), (tq, HEAD_DIM))

    qs_next = qs + tq
    q_next = jnp.concatenate([_stack_heads(qn_ref), eye], axis=1)
    s_next = _cmp_scores(q_next, kc_ref, cbias_ref, qs_next, n_blk=n_blk)

    n_chunk = WINDOW // tq + 1
    k_parts, v_parts = [], []
    for c in range(n_chunk):
        start = qs - WINDOW + c * tq
        src = pl.multiple_of(jnp.maximum(start, 0), tq)
        bias_c = wbias_ref[c * tq:(c + 1) * tq, :]
        bias_c = jnp.where(start >= 0, bias_c, jnp.full(bias_c.shape, MASK_BIAS, BF16))
        k_parts.append(jnp.concatenate([kwn_ref[pl.ds(src, tq), :], bias_c], axis=1))
        v_parts.append(jnp.concatenate([vwn_ref[pl.ds(src, tq), :], ones_q], axis=1))
    own = pl.ds(pl.multiple_of(qs, tq), tq)
    causal_bias = wbias_ref[(n_chunk - 1) * tq:n_chunk * tq, :]
    k_parts.append(jnp.concatenate([ksl_ref[own, :], causal_bias], axis=1))
    s_band = _dot_nt(q_band, jnp.concatenate(k_parts, axis=0))

    ocmp_ref[1 - slot], score_next, valid_next = _cmp_attend(s_next, vc_ref, s2c_ref, qs_next, tq=tq, n_blk=n_blk)

    s_w = s_band[:, 0:WINDOW + tq]
    p_w = jnp.exp2(s_w - jnp.max(s_w, axis=-1, keepdims=True))
    acc_w = _dot(p_w.astype(BF16), jnp.concatenate(v_parts, axis=0))
    o_win = _normalized(acc_w)
    s_own = s_band[:, WINDOW + tq:WINDOW + 2 * tq]
    m_own = jnp.max(s_own, axis=-1, keepdims=True)
    acc_own = _dot(jnp.exp2(s_own - m_own).astype(BF16), jnp.concatenate([vsl_ref[own, :], ones_q], axis=1))

    q_slc = jnp.concatenate([qst, jnp.concatenate([sel_bias] * GQA_GROUP, axis=0)], axis=1)

    def put_scores(kt, slot):
        k0 = pl.multiple_of(kt * tk, tk)
        k_aug = jnp.concatenate([ksl_ref[pl.ds(k0, tk), :], onehot_ref[pl.ds(k0, tk), :]], axis=1)
        s = _dot_nt(q_slc, k_aug)
        s_ref[slot] = s
        smax_ref[slot] = jnp.broadcast_to(jnp.max(s, axis=-1, keepdims=True), (rows, LANES))

    put_scores(0, 0)

    selb_ref[1 - slot] = _cmp_select(score_next, valid_next, tri_ref, eye_ref, qs_next, tq=tq)

    def update(kt, s, s_max, carry):
        m, acc = carry
        k0 = pl.multiple_of(kt * tk, tk)
        m_new = jnp.maximum(m, s_max)
        p = jnp.exp2(s - jnp.concatenate([m_new] * (tk // LANES), axis=1))
        alpha = jnp.concatenate([jnp.exp2(m - m_new)] * 2, axis=1)
        v_aug = jnp.concatenate([vsl_ref[pl.ds(k0, tk), :], ones_k], axis=1)
        return m_new, alpha * acc + _dot(p.astype(BF16), v_aug)

    n_kt = jnp.maximum((qs + tk - 1) // tk, 1)

    def body(j, carry):
        kt = 2 * j
        s_cur, s_max = s_ref[0], smax_ref[0]
        put_scores(kt + 1, 1)
        carry = update(kt, s_cur, s_max, carry)
        s_cur, s_max = s_ref[1], smax_ref[1]
        put_scores(jnp.minimum(kt + 2, n_kt - 1), 0)
        return update(kt + 1, s_cur, s_max, carry)

    m_init = jnp.broadcast_to(m_own, (rows, LANES))
    carry = lax.fori_loop(0, n_kt // 2, body, (m_init, acc_own))
    _, acc_s = lax.cond(n_kt % 2 == 1,
                        lambda c: update(n_kt - 1, s_ref[0], smax_ref[0], c),
                        lambda c: c, carry)
    o_slc = _normalized(acc_s)

    o_cmp = ocmp_ref[slot]
    for r in range(GQA_GROUP):
        rs = slice(r * tq, (r + 1) * tq)
        cs = slice(r * HEAD_DIM, (r + 1) * HEAD_DIM)
        y_ref[:, cs] = (gate_ref[r * N_BRANCH] * o_cmp[rs, :] + gate_ref[r * N_BRANCH + 1] * o_slc[rs, :]
                        + gate_ref[r * N_BRANCH + 2] * o_win[rs, :])


def _nsa_attention(q, k_cmp, v_cmp, ksl, vsl, kwn, vwn, cbias, s2c, tri, eye, onehot, wbias, gates):
    B, S, _ = q.shape
    n_blk = k_cmp.shape[2]
    tq = min(TQ, S)
    tk = min(TK_SLC, S)
    n_q = S // tq
    grp = GQA_GROUP * HEAD_DIM
    kernel = functools.partial(_nsa_kernel, tq=tq, tk=tk, n_blk=n_blk)
    kv = pl.BlockSpec((None, S, HEAD_DIM), lambda b, g, i: (b, 0, g))
    cmp_kv = pl.BlockSpec((None, None, n_blk, HEAD_DIM), lambda b, g, i: (b, g, 0, 0))
    const = lambda shape: pl.BlockSpec(shape, lambda b, g, i: (0, 0))
    return pl.pallas_call(
        kernel,
        grid=(B, N_KV_HEADS, S // tq),
        in_specs=[
            pl.BlockSpec((None, tq, grp), lambda b, g, i: (b, i, g)),
            pl.BlockSpec((None, tq, grp), lambda b, g, i: (b, jnp.minimum(i + 1, n_q - 1), g)),
            cmp_kv, cmp_kv, const(cbias.shape), const(s2c.shape), const((LANES, LANES)), const((tq, tq)),
            kv, kv, kv, kv,
            const((S, LANES)), const((WINDOW + tq, tq)),
            pl.BlockSpec((None, tq, LANES), lambda b, g, i: (b, i, 0)),
        ],
        out_specs=pl.BlockSpec((None, tq, grp), lambda b, g, i: (b, i, g)),
        out_shape=jax.ShapeDtypeStruct((B, S, D_ATTN), F32),
        scratch_shapes=[pltpu.VMEM((2, GQA_GROUP * tq, tk), F32),
                        pltpu.VMEM((2, GQA_GROUP * tq, LANES), F32),
                        pltpu.VMEM((2, GQA_GROUP * tq, HEAD_DIM), F32),
                        pltpu.VMEM((2, tq, LANES), BF16),
                        pltpu.VMEM((GQA_GROUP * N_BRANCH, tq, HEAD_DIM), F32)],
        compiler_params=_params(("arbitrary", "arbitrary", "arbitrary")),
        name="nsa_attention",
    )(q, q, k_cmp, v_cmp, cbias, s2c, tri, eye, ksl, vsl, kwn, vwn, onehot, wbias, gates)


def _out_proj_kernel(yc_ref, ya_ref, x_ref, w_ref, ga_ref, g1_ref, n2_ref, sh_ref, sc_ref, x1_ref, h2_ref):
    ya = _rms(ya_ref[...], ga_ref[...]).astype(BF16)
    mix = _dot(yc_ref[...], w_ref[0:D_CONV, :]) + _dot(ya, w_ref[D_CONV:D_CONV + D_ATTN, :])
    x1 = x_ref[...] + g1_ref[...] * mix
    x1_ref[...] = x1
    h2_ref[...] = (_rms(x1, n2_ref[...]) * (1.0 + sc_ref[...]) + sh_ref[...]).astype(BF16)


def _out_proj(yconv, yattn, x, w_out_b, gattn_g, mod, norm2_g):
    B, S, D = x.shape
    tm = min(TM_PROJ, S)
    tok = lambda width: pl.BlockSpec((None, tm, width), lambda b, i: (b, i, 0))
    vec = lambda width: pl.BlockSpec((1, width), lambda b, i: (0, 0))
    modspec = lambda k: pl.BlockSpec((None, None, 1, D), lambda b, i, k=k: (b, k, 0, 0))
    return pl.pallas_call(
        _out_proj_kernel,
        grid=(B, S // tm),
        in_specs=[tok(D_CONV), tok(D_ATTN), tok(D),
                  pl.BlockSpec((D_CONV + D_ATTN, D), lambda b, i: (0, 0), pipeline_mode=pl.Buffered(1)),
                  vec(D_ATTN), modspec(2), vec(D), modspec(3), modspec(4)],
        out_specs=[tok(D), tok(D)],
        out_shape=[jax.ShapeDtypeStruct((B, S, D), F32), jax.ShapeDtypeStruct((B, S, D), BF16)],
        compiler_params=_params(("arbitrary", "arbitrary")),
        name="out_proj",
    )(yconv, yattn, x, w_out_b, gattn_g, mod, norm2_g, mod, mod)


def _ffn_kernel(h_ref, w1_ref, w2_ref, x1_ref, g2_ref, nf_ref, o_ref, *, final_norm):
    f = pl.program_id(2)

    @pl.when(f == 0)
    def _():
        o_ref[...] = jnp.zeros(o_ref.shape, F32)

    a = jnp.maximum(_dot(h_ref[...], w1_ref[...]), 0.0)
    o_ref[...] += _dot((a * a).astype(BF16), w2_ref[...])

    @pl.when(f == pl.num_programs(2) - 1)
    def _():
        x2 = x1_ref[...] + g2_ref[...] * o_ref[...]
        o_ref[...] = _rms(x2, nf_ref[...]) if final_norm else x2


def _ffn(h2, w1_b, w2_b, x1, mod, normf_g, final_norm):
    B, S, D = x1.shape
    d_ff = w1_b.shape[1]
    tm = min(TM_FFN, S)
    tf = min(TF_FFN, d_ff)
    tok = pl.BlockSpec((None, tm, D), lambda b, i, f: (b, i, 0))
    return pl.pallas_call(
        functools.partial(_ffn_kernel, final_norm=final_norm),
        grid=(B, S // tm, d_ff // tf),
        in_specs=[tok,
                  pl.BlockSpec((D, tf), lambda b, i, f: (0, f)),
                  pl.BlockSpec((tf, D), lambda b, i, f: (f, 0)),
                  tok,
                  pl.BlockSpec((None, None, 1, D), lambda b, i, f: (b, 5, 0, 0)),
                  pl.BlockSpec((1, D), lambda b, i, f: (0, 0))],
        out_specs=tok,
        out_shape=jax.ShapeDtypeStruct((B, S, D), F32),
        compiler_params=_params(("arbitrary", "arbitrary", "arbitrary"), FFN_VMEM_LIMIT_BYTES),
        name="ffn",
    )(h2, w1_b, w2_b, x1, mod, normf_g)


def _rope_tables(seq):
    inv = ROPE_THETA ** (-jnp.arange(0, HEAD_DIM, 2, dtype=F32) / HEAD_DIM)
    ang = jnp.arange(seq, dtype=F32)[:, None] * inv[None, :]
    cos, sin = jnp.cos(ang), jnp.sin(ang)
    return jnp.concatenate([cos, cos], axis=-1), jnp.concatenate([-sin, sin], axis=-1)


def _slc_from_cmp(n_blk):
    sj = np.arange(LANES)[:, None]
    ci = np.arange(n_blk)[None, :]
    m = (ci * CMP_STRIDE <= sj * SLC_LEN + SLC_LEN - 1) & (ci * CMP_STRIDE + CMP_LEN - 1 >= sj * SLC_LEN)
    return jnp.asarray(np.concatenate([m, np.ones((2 * SUBLANES, n_blk), bool)], axis=0), dtype=BF16)


def _cmp_visibility_bias(n_blk, tq):
    d = np.arange(2 * n_blk)[:, None] - n_blk
    r = np.arange(tq)[None, :]
    return jnp.asarray(np.where(CMP_STRIDE * d + CMP_LEN - 1 <= r, 0.0, MASK_BIAS), dtype=F32)


def _block_onehot(seq):
    m = (np.arange(seq)[:, None] // SLC_LEN) == np.arange(LANES)[None, :]
    return jnp.asarray(m, dtype=BF16)


def _prefix_ones():
    return jnp.asarray(np.arange(LANES)[:, None] >= np.arange(LANES)[None, :], dtype=BF16)


def _window_bias(tq):
    key = np.arange(WINDOW + tq)[:, None]
    row = np.arange(tq)[None, :]
    ok = (key > row) & (key <= row + WINDOW)
    return jnp.asarray(np.where(ok, 0.0, MASK_BIAS), dtype=BF16)


def kernel(x, c, w_ada, b_ada, norm1_g, w_in, conv_w, conv_b, cmp_pe_k, cmp_pe_v, cmp_w1_k, cmp_w2_k,
           cmp_w1_v, cmp_w2_v, gnorm_conv_g, gnorm_attn_g, w_out, norm2_g, w_ff1, w_ff2, normf_g):
    B, S, D = x.shape
    depth = w_ada.shape[0]
    assert S % TQ == 0 and S // SLC_LEN <= LANES and S >= WINDOW + TQ
    assert w_in.shape[2] == D_IN
    cos_f, sin_f = _rope_tables(S)
    s2c = _slc_from_cmp(S // CMP_STRIDE)
    cbias = _cmp_visibility_bias(S // CMP_STRIDE, min(TQ, S))
    onehot = _block_onehot(S)
    tri = _prefix_ones()
    eye = jnp.eye(min(TQ, S), dtype=BF16)
    wbias = _window_bias(min(TQ, S))
    for l in range(depth):
        mod = _adaln(c, w_ada[l], b_ada[l]).reshape(B, 6, 1, D)
        w_in_b = w_in[l].astype(BF16)
        w_gate_b = jnp.pad(w_in[l][:, COL_GATE:], ((0, 0), (0, LANES - N_BRANCH * N_HEADS))).astype(BF16)
        (yconv, q, kc, vc, ksl, vsl, kwn, vwn, gates) = _in_proj(
            x, mod, norm1_g[l][None], w_in_b, w_gate_b, conv_w[l], conv_b[l][None], gnorm_conv_g[l][None], cos_f, sin_f)
        k_cmp = _compress(kc, cmp_pe_k[l], cmp_w1_k[l].astype(BF16), cmp_w2_k[l].astype(BF16))
        v_cmp = _compress(vc, cmp_pe_v[l], cmp_w1_v[l].astype(BF16), cmp_w2_v[l].astype(BF16))
        yattn = _nsa_attention(q, k_cmp, v_cmp, ksl, vsl, kwn, vwn, cbias, s2c, tri, eye, onehot, wbias, gates)
        x1, h2 = _out_proj(yconv, yattn, x, w_out[l].astype(BF16), gnorm_attn_g[l][None], mod, norm2_g[l][None])
        x = _ffn(h2, w_ff1[l].astype(BF16), w_ff2[l].astype(BF16), x1, mod, normf_g[None],
                 final_norm=(l == depth - 1))
    return x
```

```python
import functools

import numpy as np
import jax
import jax.numpy as jnp
from jax import lax
from jax.experimental import pallas as pl
from jax.experimental.pallas import tpu as pltpu

F32 = jnp.float32
BF16 = jnp.bfloat16

HEAD_DIM = 128
N_HEADS = 8
N_KV_HEADS = 2
GQA_GROUP = N_HEADS // N_KV_HEADS
D_CONV = 1024
D_ATTN = N_HEADS * HEAD_DIM
D_KV = N_KV_HEADS * HEAD_DIM
N_BRANCH = 3
CONV_WIDTH = 3
CMP_LEN = 32
CMP_STRIDE = 16
CMP_HIDDEN = 256
SLC_LEN = 64
N_SLC = 16
WINDOW = 512
ROPE_THETA = 10000.0
EPS = 1e-6
FORCE_BONUS = 1e4

LANES = 128
SUBLANES = 8
VMEM_LIMIT_BYTES = 56 * 1024 * 1024
FFN_VMEM_LIMIT_BYTES = 61 * 1024 * 1024

NEG = float(np.finfo(np.float32).min)
MASK_BIAS = -(2.0 ** 126)
LOG2E = float(np.log2(np.e))

COL_UB, COL_UC, COL_UH = 0, D_CONV, 2 * D_CONV
COL_Q = 3 * D_CONV
COL_KV = COL_Q + D_ATTN
COL_GATE = COL_KV + 2 * N_BRANCH * D_KV
D_IN = COL_GATE + N_BRANCH * N_HEADS

TM_PROJ = 512
TQ = 128
TK_SLC = 1024
TM_FFN = 512
TF_FFN = 2048
TN_ADA = 1024


def _params(sem, vmem_limit_bytes=VMEM_LIMIT_BYTES):
    return pltpu.CompilerParams(dimension_semantics=sem, vmem_limit_bytes=vmem_limit_bytes)


def _dot(a, b):
    return jnp.dot(a, b, preferred_element_type=F32)


def _dot_nt(a, b):
    return lax.dot_general(a, b, (((1,), (1,)), ((), ())), preferred_element_type=F32)


def _rms(x, g):
    return x * lax.rsqrt(jnp.mean(x * x, axis=-1, keepdims=True) + EPS) * g


def _adaln_kernel(c_ref, w_ref, b_ref, o_ref):
    c = c_ref[...]
    s = c * jax.nn.sigmoid(c)
    o_ref[...] = _dot(s.astype(BF16), w_ref[...].astype(BF16)) + b_ref[...]


def _adaln(c, w_ada, b_ada):
    B, D = c.shape
    n_out = w_ada.shape[1]
    rows = -(-B // SUBLANES) * SUBLANES
    c_pad = jnp.zeros((rows, D), F32).at[:B].set(c)
    out = pl.pallas_call(
        _adaln_kernel,
        grid=(n_out // TN_ADA,),
        in_specs=[
            pl.BlockSpec((rows, D), lambda j: (0, 0)),
            pl.BlockSpec((D, TN_ADA), lambda j: (0, j)),
            pl.BlockSpec((1, TN_ADA), lambda j: (0, j)),
        ],
        out_specs=pl.BlockSpec((rows, TN_ADA), lambda j: (0, j)),
        out_shape=jax.ShapeDtypeStruct((rows, n_out), F32),
        compiler_params=_params(("arbitrary",)),
        name="adaln",
    )(c_pad, w_ada, b_ada.reshape(1, n_out))
    return out[:B]


def _rope(u, cos_f, sin_f):
    return u * cos_f + pltpu.roll(u, HEAD_DIM // 2, 1) * sin_f


def _in_proj_kernel(x_ref, sh_ref, sc_ref, g_ref, w_ref, wg_ref, cw_ref, cb_ref, gc_ref, cos_ref, sin_ref,
                    yconv_ref, q_ref, kc_ref, vc_ref, ksl_ref, vsl_ref, kwn_ref, vwn_ref, gate_ref,
                    vbuf_ref, *, tm, q_scale):
    @pl.when(pl.program_id(1) == 0)
    def _():
        vbuf_ref[tm:tm + SUBLANES, :] = jnp.zeros((SUBLANES, D_CONV), F32)

    x = x_ref[...]
    h = _rms(x, g_ref[...]) * (1.0 + sc_ref[...]) + sh_ref[...]
    hb = h.astype(BF16)

    ub = _dot(hb, w_ref[:, COL_UB:COL_UB + D_CONV])
    uc = _dot(hb, w_ref[:, COL_UC:COL_UC + D_CONV])
    uh = _dot(hb, w_ref[:, COL_UH:COL_UH + D_CONV])
    v = uc * uh
    vbuf_ref[0:SUBLANES, :] = vbuf_ref[tm:tm + SUBLANES, :]
    vbuf_ref[SUBLANES:SUBLANES + tm, :] = v
    v1 = vbuf_ref[SUBLANES - 1:SUBLANES - 1 + tm, :]
    v2 = vbuf_ref[SUBLANES - 2:SUBLANES - 2 + tm, :]
    z = cb_ref[...] + cw_ref[0:1, :] * v2 + cw_ref[1:2, :] * v1 + cw_ref[2:3, :] * v
    yconv_ref[...] = _rms(ub * z, gc_ref[...]).astype(BF16)

    cos_f = cos_ref[...]
    sin_f = sin_ref[...]

    uq = _dot(hb, w_ref[:, COL_Q:COL_Q + D_ATTN])
    for hd in range(N_HEADS):
        sl = slice(hd * HEAD_DIM, (hd + 1) * HEAD_DIM)
        q_ref[:, sl] = (_rope(uq[:, sl], cos_f, sin_f) * q_scale).astype(BF16)

    ug = _dot(hb, wg_ref[...])
    gate_ref[...] = jax.nn.sigmoid(ug)

    outs = (kc_ref, vc_ref, ksl_ref, vsl_ref, kwn_ref, vwn_ref)
    for n in (0, 2, 4, 1, 3, 5):
        u = _dot(hb, w_ref[:, COL_KV + n * D_KV:COL_KV + (n + 1) * D_KV])
        for g in range(N_KV_HEADS):
            cs = slice(g * HEAD_DIM, (g + 1) * HEAD_DIM)
            t = _rope(u[:, cs], cos_f, sin_f) if n % 2 == 0 else u[:, cs]
            outs[n][:, cs] = t.astype(outs[n].dtype)


def _in_proj(x, mod, norm1_g, w_in_b, w_gate_b, conv_w, conv_b, gconv_g, cos_f, sin_f):
    B, S, D = x.shape
    tm = min(TM_PROJ, S)
    tok = lambda width: pl.BlockSpec((None, tm, width), lambda b, i: (b, i, 0))
    vec = lambda width: pl.BlockSpec((1, width), lambda b, i: (0, 0))
    modspec = lambda k: pl.BlockSpec((None, None, 1, D), lambda b, i, k=k: (b, k, 0, 0))
    kernel = functools.partial(_in_proj_kernel, tm=tm, q_scale=LOG2E * HEAD_DIM ** -0.5)
    sd = jax.ShapeDtypeStruct
    return pl.pallas_call(
        kernel,
        grid=(B, S // tm),
        in_specs=[
            tok(D), modspec(0), modspec(1), vec(D),
            pl.BlockSpec((D, D_IN), lambda b, i: (0, 0), pipeline_mode=pl.Buffered(1)),
            pl.BlockSpec((D, LANES), lambda b, i: (0, 0)),
            pl.BlockSpec((CONV_WIDTH, D_CONV), lambda b, i: (0, 0)), vec(D_CONV), vec(D_CONV),
            pl.BlockSpec((tm, HEAD_DIM), lambda b, i: (i, 0)),
            pl.BlockSpec((tm, HEAD_DIM), lambda b, i: (i, 0)),
        ],
        out_specs=[tok(D_CONV), tok(D_ATTN), tok(D_KV), tok(D_KV), tok(D_KV), tok(D_KV), tok(D_KV),
                   tok(D_KV), tok(LANES)],
        out_shape=[sd((B, S, D_CONV), BF16), sd((B, S, D_ATTN), BF16),
                   sd((B, S, D_KV), F32), sd((B, S, D_KV), F32),
                   sd((B, S, D_KV), BF16), sd((B, S, D_KV), BF16),
                   sd((B, S, D_KV), BF16), sd((B, S, D_KV), BF16),
                   sd((B, S, LANES), F32)],
        scratch_shapes=[pltpu.VMEM((tm + 2 * SUBLANES, D_CONV), F32)],
        compiler_params=_params(("arbitrary", "arbitrary")),
        name="in_proj",
    )(x, mod, mod, norm1_g, w_in_b, w_gate_b, conv_w, conv_b, gconv_g, cos_f, sin_f)


def _compress_kernel(kv_ref, pe_ref, w1_ref, w2_ref, o_ref, buf_ref, *, seq, n_blk):
    buf_ref[0:seq, :] = kv_ref[...]
    buf_ref[seq:seq + CMP_STRIDE, :] = jnp.zeros((CMP_STRIDE, HEAD_DIM), F32)
    def token(l):
        return (buf_ref[pl.ds(l, n_blk, stride=CMP_STRIDE), :] + pe_ref[l:l + 1, :]).astype(BF16)

    acc = jnp.zeros((n_blk, CMP_HIDDEN), F32)
    for l in range(0, CMP_LEN, 2):
        pair = jnp.concatenate([token(l), token(l + 1)], axis=1)
        acc = acc + _dot(pair, w1_ref[l * HEAD_DIM:(l + 2) * HEAD_DIM, :])
    hid = jax.nn.gelu(acc)
    o_ref[...] = _dot(hid.astype(BF16), w2_ref[...]).astype(BF16)


def _compress(kv, pe, w1_b, w2_b):
    B, S, _ = kv.shape
    n_blk = S // CMP_STRIDE
    kernel = functools.partial(_compress_kernel, seq=S, n_blk=n_blk)
    return pl.pallas_call(
        kernel,
        grid=(B, N_KV_HEADS),
        in_specs=[
            pl.BlockSpec((None, S, HEAD_DIM), lambda b, g: (b, 0, g)),
            pl.BlockSpec((CMP_LEN, HEAD_DIM), lambda b, g: (0, 0)),
            pl.BlockSpec((CMP_LEN * HEAD_DIM, CMP_HIDDEN), lambda b, g: (0, 0)),
            pl.BlockSpec((CMP_HIDDEN, HEAD_DIM), lambda b, g: (0, 0)),
        ],
        out_specs=pl.BlockSpec((None, None, n_blk, HEAD_DIM), lambda b, g: (b, g, 0, 0)),
        out_shape=jax.ShapeDtypeStruct((B, N_KV_HEADS, n_blk, HEAD_DIM), BF16),
        scratch_shapes=[pltpu.VMEM((S + CMP_STRIDE, HEAD_DIM), F32)],
        compiler_params=_params(("arbitrary", "arbitrary")),
        name="compress",
    )(kv, pe, w1_b, w2_b)


def _stack_heads(q_ref):
    return jnp.concatenate([q_ref[:, r * HEAD_DIM:(r + 1) * HEAD_DIM] for r in range(GQA_GROUP)], axis=0)


def _normalized(acc):
    l = acc[:, HEAD_DIM:2 * HEAD_DIM]
    return acc[:, 0:HEAD_DIM] / jnp.where(l > 0.0, l, 1.0)


def _gate_col(gates, g, r, branch):
    lo = r * N_BRANCH + branch
    hi = (GQA_GROUP + r) * N_BRANCH + branch
    return jnp.where(g == 0, gates[:, lo:lo + 1], gates[:, hi:hi + 1])


def _cmp_scores(q_band, kc_ref, cbias_ref, qs, *, n_blk):
    start = pl.multiple_of(n_blk - qs // CMP_STRIDE, SUBLANES)
    vis_bias = cbias_ref[pl.ds(start, n_blk), :].astype(BF16)
    return _dot_nt(q_band, jnp.concatenate([kc_ref[...], vis_bias], axis=1))


def _cmp_attend(s, vc_ref, s2c_ref, qs, *, tq, n_blk):
    rows = GQA_GROUP * tq
    p = jnp.exp2(s - jnp.max(s, axis=-1, keepdims=True)).astype(BF16)

    imp = jnp.zeros((LANES, tq), F32)
    for r in range(GQA_GROUP):
        it = _dot_nt(s2c_ref[...], p[r * tq:(r + 1) * tq, :])
        l_t = it[LANES:LANES + 1, :]
        imp = imp + it[0:LANES, :] / jnp.where(l_t > 0.0, l_t, 1.0)

    ov = _dot(p, jnp.concatenate([vc_ref[...], jnp.ones((n_blk, HEAD_DIM), BF16)], axis=1))
    t_row = qs + (lax.broadcasted_iota(jnp.int32, (rows, HEAD_DIM), 0) & (tq - 1))
    o_cmp = jnp.where(t_row >= CMP_LEN - 1, _normalized(ov), 0.0)
    t = qs + lax.broadcasted_iota(jnp.int32, (LANES, tq), 1)
    imp = jnp.where(t >= CMP_LEN - 1, imp, 0.0)
    blk = lax.broadcasted_iota(jnp.int32, (LANES, tq), 0)
    cur = t // SLC_LEN
    valid = blk * SLC_LEN <= t
    forced = (blk == 0) | (blk == cur) | (blk == cur - 1)
    return o_cmp, jnp.where(valid, imp + jnp.where(forced, FORCE_BONUS, 0.0), -1.0), valid


def _cmp_select(score, valid, tri_ref, eye_ref, qs, *, tq):
    x = score
    covered = jnp.zeros((1, tq), F32)
    n_before = jnp.zeros((1, tq), F32)
    theta = jnp.full((1, tq), -1.0, F32)
    for _ in range(N_SLC):
        mx = jnp.max(x, axis=0, keepdims=True)
        eq = x == mx
        cnt = jnp.sum(jnp.where(eq, 1.0, 0.0), axis=0, keepdims=True)
        crossing = (covered < float(N_SLC)) & (covered + cnt >= float(N_SLC))
        theta = jnp.where(crossing, mx, theta)
        n_before = jnp.where(crossing, covered, n_before)
        covered = covered + cnt
        x = jnp.where(eq, -jnp.inf, x)
    tie = score == theta
    tie_rank = _dot(tri_ref[...], jnp.where(tie, 1.0, 0.0).astype(BF16))
    chosen = (score > theta) | (tie & (tie_rank <= float(N_SLC) - n_before))
    sel_t = jnp.where(valid & chosen, 1.0, 0.0).astype(BF16)
    sel = _dot_nt(eye_ref[...], sel_t)
    below = lax.broadcasted_iota(jnp.int32, (tq, LANES), 1) < qs // SLC_LEN
    return jnp.where(below, (sel - 1.0) * (-MASK_BIAS), MASK_BIAS).astype(BF16)


def _nsa_kernel(q_ref, qn_ref, kc_ref, vc_ref, cbias_ref, s2c_ref, tri_ref, eye_ref, ksl_ref, vsl_ref, kwn_ref,
                vwn_ref, onehot_ref, wbias_ref, gates_in_ref, y_ref, s_ref, smax_ref, ocmp_ref, selb_ref, gate_ref,
                *, tq, tk, n_blk):
    g = pl.program_id(1)
    i = pl.program_id(2)
    qs = i * tq
    slot = i & 1
    rows = GQA_GROUP * tq
    row_id = lax.broadcasted_iota(jnp.int32, (rows, tq), 0) & (tq - 1)
    eye = jnp.where(row_id == lax.broadcasted_iota(jnp.int32, (rows, tq), 1), 1.0, 0.0).astype(BF16)

    @pl.when(i == 0)
    def _():
        q0 = jnp.concatenate([_stack_heads(q_ref), eye], axis=1)
        s0 = _cmp_scores(q0, kc_ref, cbias_ref, qs, n_blk=n_blk)
        ocmp_ref[0], score0, valid0 = _cmp_attend(s0, vc_ref, s2c_ref, qs, tq=tq, n_blk=n_blk)
        selb_ref[0] = _cmp_select(score0, valid0, tri_ref, eye_ref, qs, tq=tq)

    qst = _stack_heads(q_ref)
    ones_k = jnp.ones((tk, HEAD_DIM), BF16)
    ones_q = jnp.ones((tq, HEAD_DIM), BF16)
    q_band = jnp.concatenate([qst, eye], axis=1)
    sel_bias = selb_ref[slot]
    gates = gates_in_ref[...]
    for r in range(GQA_GROUP):
        for br in range(N_BRANCH):
            gate_ref[r * N_BRANCH + br] = jnp.broadcast_to(_gate_col(gates, g, r, br), (tq, HEAD_DIM))

    qs_next = qs + tq
    q_next = jnp.concatenate([_stack_heads(qn_ref), eye], axis=1)
    s_next = _cmp_scores(q_next, kc_ref, cbias_ref, qs_next, n_blk=n_blk)

    n_chunk = WINDOW // tq + 1
    k_parts, v_parts = [], []
    for c in range(n_chunk):
        start = qs - WINDOW + c * tq
        src = pl.multiple_of(jnp.maximum(start, 0), tq)
        bias_c = wbias_ref[c * tq:(c + 1) * tq, :]
        bias_c = jnp.where(start >= 0, bias_c, jnp.full(bias_c.shape, MASK_BIAS, BF16))
        k_parts.append(jnp.concatenate([kwn_ref[pl.ds(src, tq), :], bias_c], axis=1))
        v_parts.append(jnp.concatenate([vwn_ref[pl.ds(src, tq), :], ones_q], axis=1))
    own = pl.ds(pl.multiple_of(qs, tq), tq)
    causal_bias = wbias_ref[(n_chunk - 1) * tq:n_chunk * tq, :]
    k_parts.append(jnp.concatenate([ksl_ref[own, :], causal_bias], axis=1))
    s_band = _dot_nt(q_band, jnp.concatenate(k_parts, axis=0))

    ocmp_ref[1 - slot], score_next, valid_next = _cmp_attend(s_next, vc_ref, s2c_ref, qs_next, tq=tq, n_blk=n_blk)

    s_w = s_band[:, 0:WINDOW + tq]
    p_w = jnp.exp2(s_w - jnp.max(s_w, axis=-1, keepdims=True))
    acc_w = _dot(p_w.astype(BF16), jnp.concatenate(v_parts, axis=0))
    o_win = _normalized(acc_w)
    s_own = s_band[:, WINDOW + tq:WINDOW + 2 * tq]
    m_own = jnp.max(s_own, axis=-1, keepdims=True)
    acc_own = _dot(jnp.exp2(s_own - m_own).astype(BF16), jnp.concatenate([vsl_ref[own, :], ones_q], axis=1))

    q_slc = jnp.concatenate([qst, jnp.concatenate([sel_bias] * GQA_GROUP, axis=0)], axis=1)

    def put_scores(kt, slot):
        k0 = pl.multiple_of(kt * tk, tk)
        k_aug = jnp.concatenate([ksl_ref[pl.ds(k0, tk), :], onehot_ref[pl.ds(k0, tk), :]], axis=1)
        s = _dot_nt(q_slc, k_aug)
        s_ref[slot] = s
        smax_ref[slot] = jnp.broadcast_to(jnp.max(s, axis=-1, keepdims=True), (rows, LANES))

    put_scores(0, 0)

    selb_ref[1 - slot] = _cmp_select(score_next, valid_next, tri_ref, eye_ref, qs_next, tq=tq)

    def update(kt, s, s_max, carry):
        m, acc = carry
        k0 = pl.multiple_of(kt * tk, tk)
        m_new = jnp.maximum(m, s_max)
        p = jnp.exp2(s - jnp.concatenate([m_new] * (tk // LANES), axis=1))
        alpha = jnp.concatenate([jnp.exp2(m - m_new)] * 2, axis=1)
        v_aug = jnp.concatenate([vsl_ref[pl.ds(k0, tk), :], ones_k], axis=1)
        return m_new, alpha * acc + _dot(p.astype(BF16), v_aug)

    n_kt = jnp.maximum((qs + tk - 1) // tk, 1)

    def body(j, carry):
        kt = 2 * j
        s_cur, s_max = s_ref[0], smax_ref[0]
        put_scores(kt + 1, 1)
        carry = update(kt, s_cur, s_max, carry)
        s_cur, s_max = s_ref[1], smax_ref[1]
        put_scores(jnp.minimum(kt + 2, n_kt - 1), 0)
        return update(kt + 1, s_cur, s_max, carry)

    m_init = jnp.broadcast_to(m_own, (rows, LANES))
    carry = lax.fori_loop(0, n_kt // 2, body, (m_init, acc_own))
    _, acc_s = lax.cond(n_kt % 2 == 1,
                        lambda c: update(n_kt - 1, s_ref[0], smax_ref[0], c),
                        lambda c: c, carry)
    o_slc = _normalized(acc_s)

    o_cmp = ocmp_ref[slot]
    for r in range(GQA_GROUP):
        rs = slice(r * tq, (r + 1) * tq)
        cs = slice(r * HEAD_DIM, (r + 1) * HEAD_DIM)
        y_ref[:, cs] = (gate_ref[r * N_BRANCH] * o_cmp[rs, :] + gate_ref[r * N_BRANCH + 1] * o_slc[rs, :]
                        + gate_ref[r * N_BRANCH + 2] * o_win[rs, :])


def _nsa_attention(q, k_cmp, v_cmp, ksl, vsl, kwn, vwn, cbias, s2c, tri, eye, onehot, wbias, gates):
    B, S, _ = q.shape
    n_blk = k_cmp.shape[2]
    tq = min(TQ, S)
    tk = min(TK_SLC, S)
    n_q = S // tq
    grp = GQA_GROUP * HEAD_DIM
    kernel = functools.partial(_nsa_kernel, tq=tq, tk=tk, n_blk=n_blk)
    kv = pl.BlockSpec((None, S, HEAD_DIM), lambda b, g, i: (b, 0, g))
    cmp_kv = pl.BlockSpec((None, None, n_blk, HEAD_DIM), lambda b, g, i: (b, g, 0, 0))
    const = lambda shape: pl.BlockSpec(shape, lambda b, g, i: (0, 0))
    return pl.pallas_call(
        kernel,
        grid=(B, N_KV_HEADS, S // tq),
        in_specs=[
            pl.BlockSpec((None, tq, grp), lambda b, g, i: (b, i, g)),
            pl.BlockSpec((None, tq, grp), lambda b, g, i: (b, jnp.minimum(i + 1, n_q - 1), g)),
            cmp_kv, cmp_kv, const(cbias.shape), const(s2c.shape), const((LANES, LANES)), const((tq, tq)),
            kv, kv, kv, kv,
            const((S, LANES)), const((WINDOW + tq, tq)),
            pl.BlockSpec((None, tq, LANES), lambda b, g, i: (b, i, 0)),
        ],
        out_specs=pl.BlockSpec((None, tq, grp), lambda b, g, i: (b, i, g)),
        out_shape=jax.ShapeDtypeStruct((B, S, D_ATTN), F32),
        scratch_shapes=[pltpu.VMEM((2, GQA_GROUP * tq, tk), F32),
                        pltpu.VMEM((2, GQA_GROUP * tq, LANES), F32),
                        pltpu.VMEM((2, GQA_GROUP * tq, HEAD_DIM), F32),
                        pltpu.VMEM((2, tq, LANES), BF16),
                        pltpu.VMEM((GQA_GROUP * N_BRANCH, tq, HEAD_DIM), F32)],
        compiler_params=_params(("arbitrary", "arbitrary", "arbitrary")),
        name="nsa_attention",
    )(q, q, k_cmp, v_cmp, cbias, s2c, tri, eye, ksl, vsl, kwn, vwn, onehot, wbias, gates)


def _out_proj_kernel(yc_ref, ya_ref, x_ref, w_ref, ga_ref, g1_ref, n2_ref, sh_ref, sc_ref, x1_ref, h2_ref):
    ya = _rms(ya_ref[...], ga_ref[...]).astype(BF16)
    mix = _dot(yc_ref[...], w_ref[0:D_CONV, :]) + _dot(ya, w_ref[D_CONV:D_CONV + D_ATTN, :])
    x1 = x_ref[...] + g1_ref[...] * mix
    x1_ref[...] = x1
    h2_ref[...] = (_rms(x1, n2_ref[...]) * (1.0 + sc_ref[...]) + sh_ref[...]).astype(BF16)


def _out_proj(yconv, yattn, x, w_out_b, gattn_g, mod, norm2_g):
    B, S, D = x.shape
    tm = min(TM_PROJ, S)
    tok = lambda width: pl.BlockSpec((None, tm, width), lambda b, i: (b, i, 0))
    vec = lambda width: pl.BlockSpec((1, width), lambda b, i: (0, 0))
    modspec = lambda k: pl.BlockSpec((None, None, 1, D), lambda b, i, k=k: (b, k, 0, 0))
    return pl.pallas_call(
        _out_proj_kernel,
        grid=(B, S // tm),
        in_specs=[tok(D_CONV), tok(D_ATTN), tok(D),
                  pl.BlockSpec((D_CONV + D_ATTN, D), lambda b, i: (0, 0), pipeline_mode=pl.Buffered(1)),
                  vec(D_ATTN), modspec(2), vec(D), modspec(3), modspec(4)],
        out_specs=[tok(D), tok(D)],
        out_shape=[jax.ShapeDtypeStruct((B, S, D), F32), jax.ShapeDtypeStruct((B, S, D), BF16)],
        compiler_params=_params(("arbitrary", "arbitrary")),
        name="out_proj",
    )(yconv, yattn, x, w_out_b, gattn_g, mod, norm2_g, mod, mod)


def _ffn_kernel(h_ref, w1_ref, w2_ref, x1_ref, g2_ref, nf_ref, o_ref, *, final_norm):
    f = pl.program_id(2)

    @pl.when(f == 0)
    def _():
        o_ref[...] = jnp.zeros(o_ref.shape, F32)

    a = jnp.maximum(_dot(h_ref[...], w1_ref[...]), 0.0)
    o_ref[...] += _dot((a * a).astype(BF16), w2_ref[...])

    @pl.when(f == pl.num_programs(2) - 1)
    def _():
        x2 = x1_ref[...] + g2_ref[...] * o_ref[...]
        o_ref[...] = _rms(x2, nf_ref[...]) if final_norm else x2


def _ffn(h2, w1_b, w2_b, x1, mod, normf_g, final_norm):
    B, S, D = x1.shape
    d_ff = w1_b.shape[1]
    tm = min(TM_FFN, S)
    tf = min(TF_FFN, d_ff)
    tok = pl.BlockSpec((None, tm, D), lambda b, i, f: (b, i, 0))
    return pl.pallas_call(
        functools.partial(_ffn_kernel, final_norm=final_norm),
        grid=(B, S // tm, d_ff // tf),
        in_specs=[tok,
                  pl.BlockSpec((D, tf), lambda b, i, f: (0, f)),
                  pl.BlockSpec((tf, D), lambda b, i, f: (f, 0)),
                  tok,
                  pl.BlockSpec((None, None, 1, D), lambda b, i, f: (b, 5, 0, 0)),
                  pl.BlockSpec((1, D), lambda b, i, f: (0, 0))],
        out_specs=tok,
        out_shape=jax.ShapeDtypeStruct((B, S, D), F32),
        compiler_params=_params(("arbitrary", "arbitrary", "arbitrary"), FFN_VMEM_LIMIT_BYTES),
        name="ffn",
    )(h2, w1_b, w2_b, x1, mod, normf_g)


def _rope_tables(seq):
    inv = ROPE_THETA ** (-jnp.arange(0, HEAD_DIM, 2, dtype=F32) / HEAD_DIM)
    ang = jnp.arange(seq, dtype=F32)[:, None] * inv[None, :]
    cos, sin = jnp.cos(ang), jnp.sin(ang)
    return jnp.concatenate([cos, cos], axis=-1), jnp.concatenate([-sin, sin], axis=-1)


def _slc_from_cmp(n_blk):
    sj = np.arange(LANES)[:, None]
    ci = np.arange(n_blk)[None, :]
    m = (ci * CMP_STRIDE <= sj * SLC_LEN + SLC_LEN - 1) & (ci * CMP_STRIDE + CMP_LEN - 1 >= sj * SLC_LEN)
    return jnp.asarray(np.concatenate([m, np.ones((2 * SUBLANES, n_blk), bool)], axis=0), dtype=BF16)


def _cmp_visibility_bias(n_blk, tq):
    d = np.arange(2 * n_blk)[:, None] - n_blk
    r = np.arange(tq)[None, :]
    return jnp.asarray(np.where(CMP_STRIDE * d + CMP_LEN - 1 <= r, 0.0, MASK_BIAS), dtype=F32)


def _block_onehot(seq):
    m = (np.arange(seq)[:, None] // SLC_LEN) == np.arange(LANES)[None, :]
    return jnp.asarray(m, dtype=BF16)


def _prefix_ones():
    return jnp.asarray(np.arange(LANES)[:, None] >= np.arange(LANES)[None, :], dtype=BF16)


def _window_bias(tq):
    key = np.arange(WINDOW + tq)[:, None]
    row = np.arange(tq)[None, :]
    ok = (key > row) & (key <= row + WINDOW)
    return jnp.asarray(np.where(ok, 0.0, MASK_BIAS), dtype=BF16)


def kernel(x, c, w_ada, b_ada, norm1_g, w_in, conv_w, conv_b, cmp_pe_k, cmp_pe_v, cmp_w1_k, cmp_w2_k,
           cmp_w1_v, cmp_w2_v, gnorm_conv_g, gnorm_attn_g, w_out, norm2_g, w_ff1, w_ff2, normf_g):
    B, S, D = x.shape
    depth = w_ada.shape[0]
    assert S % TQ == 0 and S // SLC_LEN <= LANES and S >= WINDOW + TQ
    assert w_in.shape[2] == D_IN
    cos_f, sin_f = _rope_tables(S)
    s2c = _slc_from_cmp(S // CMP_STRIDE)
    cbias = _cmp_visibility_bias(S // CMP_STRIDE, min(TQ, S))
    onehot = _block_onehot(S)
    tri = _prefix_ones()
    eye = jnp.eye(min(TQ, S), dtype=BF16)
    wbias = _window_bias(min(TQ, S))
    for l in range(depth):
        mod = _adaln(c, w_ada[l], b_ada[l]).reshape(B, 6, 1, D)
        w_in_b = w_in[l].astype(BF16)
        w_gate_b = jnp.pad(w_in[l][:, COL_GATE:], ((0, 0), (0, LANES - N_BRANCH * N_HEADS))).astype(BF16)
        (yconv, q, kc, vc, ksl, vsl, kwn, vwn, gates) = _in_proj(
            x, mod, norm1_g[l][None], w_in_b, w_gate_b, conv_w[l], conv_b[l][None], gnorm_conv_g[l][None], cos_f, sin_f)
        k_cmp = _compress(kc, cmp_pe_k[l], cmp_w1_k[l].astype(BF16), cmp_w2_k[l].astype(BF16))
        v_cmp = _compress(vc, cmp_pe_v[l], cmp_w1_v[l].astype(BF16), cmp_w2_v[l].astype(BF16))
        yattn = _nsa_attention(q, k_cmp, v_cmp, ksl, vsl, kwn, vwn, cbias, s2c, tri, eye, onehot, wbias, gates)
        x1, h2 = _out_proj(yconv, yattn, x, w_out[l].astype(BF16), gnorm_attn_g[l][None], mod, norm2_g[l][None])
        x = _ffn(h2, w_ff1[l].astype(BF16), w_ff2[l].astype(BF16), x1, mod, normf_g[None],
                 final_norm=(l == depth - 1))
    return x
```

```python
import functools

import numpy as np
import jax
import jax.numpy as jnp
from jax import lax
from jax.experimental import pallas as pl
from jax.experimental.pallas import tpu as pltpu

F32 = jnp.float32
BF16 = jnp.bfloat16

HEAD_DIM = 128
N_HEADS = 8
N_KV_HEADS = 2
GQA_GROUP = N_HEADS // N_KV_HEADS
D_CONV = 1024
D_ATTN = N_HEADS * HEAD_DIM
D_KV = N_KV_HEADS * HEAD_DIM
N_BRANCH = 3
CONV_WIDTH = 3
CMP_LEN = 32
CMP_STRIDE = 16
CMP_HIDDEN = 256
SLC_LEN = 64
N_SLC = 16
WINDOW = 512
ROPE_THETA = 10000.0
EPS = 1e-6
FORCE_BONUS = 1e4

LANES = 128
SUBLANES = 8
VMEM_LIMIT_BYTES = 56 * 1024 * 1024
FFN_VMEM_LIMIT_BYTES = 61 * 1024 * 1024

NEG = float(np.finfo(np.float32).min)
MASK_BIAS = -(2.0 ** 126)
LOG2E = float(np.log2(np.e))

COL_UB, COL_UC, COL_UH = 0, D_CONV, 2 * D_CONV
COL_Q = 3 * D_CONV
COL_KV = COL_Q + D_ATTN
COL_GATE = COL_KV + 2 * N_BRANCH * D_KV
D_IN = COL_GATE + N_BRANCH * N_HEADS

TM_PROJ = 512
TQ = 128
TILES_PER_STEP = 2
TK_SLC = 1024
TM_FFN = 512
TF_FFN = 2048
TN_ADA = 1024


def _params(sem, vmem_limit_bytes=VMEM_LIMIT_BYTES):
    return pltpu.CompilerParams(dimension_semantics=sem, vmem_limit_bytes=vmem_limit_bytes)


def _dot(a, b):
    return jnp.dot(a, b, preferred_element_type=F32)


def _dot_nt(a, b):
    return lax.dot_general(a, b, (((1,), (1,)), ((), ())), preferred_element_type=F32)


def _rms(x, g):
    return x * lax.rsqrt(jnp.mean(x * x, axis=-1, keepdims=True) + EPS) * g


def _adaln_kernel(c_ref, w_ref, b_ref, o_ref):
    c = c_ref[...]
    s = c * jax.nn.sigmoid(c)
    o_ref[...] = _dot(s.astype(BF16), w_ref[...].astype(BF16)) + b_ref[...]


def _adaln(c, w_ada, b_ada):
    B, D = c.shape
    n_out = w_ada.shape[1]
    rows = -(-B // SUBLANES) * SUBLANES
    c_pad = jnp.zeros((rows, D), F32).at[:B].set(c)
    out = pl.pallas_call(
        _adaln_kernel,
        grid=(n_out // TN_ADA,),
        in_specs=[
            pl.BlockSpec((rows, D), lambda j: (0, 0)),
            pl.BlockSpec((D, TN_ADA), lambda j: (0, j)),
            pl.BlockSpec((1, TN_ADA), lambda j: (0, j)),
        ],
        out_specs=pl.BlockSpec((rows, TN_ADA), lambda j: (0, j)),
        out_shape=jax.ShapeDtypeStruct((rows, n_out), F32),
        compiler_params=_params(("arbitrary",)),
        name="adaln",
    )(c_pad, w_ada, b_ada.reshape(1, n_out))
    return out[:B]


def _rope(u, cos_f, sin_f):
    return u * cos_f + pltpu.roll(u, HEAD_DIM // 2, 1) * sin_f


def _in_proj_kernel(x_ref, sh_ref, sc_ref, g_ref, w_ref, wg_ref, cw_ref, cb_ref, gc_ref, cos_ref, sin_ref,
                    yconv_ref, q_ref, kc_ref, vc_ref, ksl_ref, vsl_ref, kwn_ref, vwn_ref, gate_ref,
                    vbuf_ref, *, tm, q_scale):
    @pl.when(pl.program_id(1) == 0)
    def _():
        vbuf_ref[tm:tm + SUBLANES, :] = jnp.zeros((SUBLANES, D_CONV), F32)

    x = x_ref[...]
    h = _rms(x, g_ref[...]) * (1.0 + sc_ref[...]) + sh_ref[...]
    hb = h.astype(BF16)

    ub = _dot(hb, w_ref[:, COL_UB:COL_UB + D_CONV])
    uc = _dot(hb, w_ref[:, COL_UC:COL_UC + D_CONV])
    uh = _dot(hb, w_ref[:, COL_UH:COL_UH + D_CONV])
    v = uc * uh
    vbuf_ref[0:SUBLANES, :] = vbuf_ref[tm:tm + SUBLANES, :]
    vbuf_ref[SUBLANES:SUBLANES + tm, :] = v
    v1 = vbuf_ref[SUBLANES - 1:SUBLANES - 1 + tm, :]
    v2 = vbuf_ref[SUBLANES - 2:SUBLANES - 2 + tm, :]
    z = cb_ref[...] + cw_ref[0:1, :] * v2 + cw_ref[1:2, :] * v1 + cw_ref[2:3, :] * v
    yconv_ref[...] = _rms(ub * z, gc_ref[...]).astype(BF16)

    cos_f = cos_ref[...]
    sin_f = sin_ref[...]

    uq = _dot(hb, w_ref[:, COL_Q:COL_Q + D_ATTN])
    for hd in range(N_HEADS):
        sl = slice(hd * HEAD_DIM, (hd + 1) * HEAD_DIM)
        q_ref[:, sl] = (_rope(uq[:, sl], cos_f, sin_f) * q_scale).astype(BF16)

    ug = _dot(hb, wg_ref[...])
    gate_ref[...] = jax.nn.sigmoid(ug)

    outs = (kc_ref, vc_ref, ksl_ref, vsl_ref, kwn_ref, vwn_ref)
    for n in (0, 2, 4, 1, 3, 5):
        u = _dot(hb, w_ref[:, COL_KV + n * D_KV:COL_KV + (n + 1) * D_KV])
        for g in range(N_KV_HEADS):
            cs = slice(g * HEAD_DIM, (g + 1) * HEAD_DIM)
            t = _rope(u[:, cs], cos_f, sin_f) if n % 2 == 0 else u[:, cs]
            outs[n][:, cs] = t.astype(outs[n].dtype)


def _in_proj(x, mod, norm1_g, w_in_b, w_gate_b, conv_w, conv_b, gconv_g, cos_f, sin_f):
    B, S, D = x.shape
    tm = min(TM_PROJ, S)
    tok = lambda width: pl.BlockSpec((None, tm, width), lambda b, i: (b, i, 0))
    vec = lambda width: pl.BlockSpec((1, width), lambda b, i: (0, 0))
    modspec = lambda k: pl.BlockSpec((None, None, 1, D), lambda b, i, k=k: (b, k, 0, 0))
    kernel = functools.partial(_in_proj_kernel, tm=tm, q_scale=LOG2E * HEAD_DIM ** -0.5)
    sd = jax.ShapeDtypeStruct
    return pl.pallas_call(
        kernel,
        grid=(B, S // tm),
        in_specs=[
            tok(D), modspec(0), modspec(1), vec(D),
            pl.BlockSpec((D, D_IN), lambda b, i: (0, 0), pipeline_mode=pl.Buffered(1)),
            pl.BlockSpec((D, LANES), lambda b, i: (0, 0)),
            pl.BlockSpec((CONV_WIDTH, D_CONV), lambda b, i: (0, 0)), vec(D_CONV), vec(D_CONV),
            pl.BlockSpec((tm, HEAD_DIM), lambda b, i: (i, 0)),
            pl.BlockSpec((tm, HEAD_DIM), lambda b, i: (i, 0)),
        ],
        out_specs=[tok(D_CONV), tok(D_ATTN), tok(D_KV), tok(D_KV), tok(D_KV), tok(D_KV), tok(D_KV),
                   tok(D_KV), tok(LANES)],
        out_shape=[sd((B, S, D_CONV), BF16), sd((B, S, D_ATTN), BF16),
                   sd((B, S, D_KV), F32), sd((B, S, D_KV), F32),
                   sd((B, S, D_KV), BF16), sd((B, S, D_KV), BF16),
                   sd((B, S, D_KV), BF16), sd((B, S, D_KV), BF16),
                   sd((B, S, LANES), F32)],
        scratch_shapes=[pltpu.VMEM((tm + 2 * SUBLANES, D_CONV), F32)],
        compiler_params=_params(("arbitrary", "arbitrary")),
        name="in_proj",
    )(x, mod, mod, norm1_g, w_in_b, w_gate_b, conv_w, conv_b, gconv_g, cos_f, sin_f)


def _compress_kernel(kv_ref, pe_ref, w1_ref, w2_ref, o_ref, buf_ref, *, seq, n_blk):
    buf_ref[0:seq, :] = kv_ref[...]
    buf_ref[seq:seq + CMP_STRIDE, :] = jnp.zeros((CMP_STRIDE, HEAD_DIM), F32)
    def token(l):
        return (buf_ref[pl.ds(l, n_blk, stride=CMP_STRIDE), :] + pe_ref[l:l + 1, :]).astype(BF16)

    acc = jnp.zeros((n_blk, CMP_HIDDEN), F32)
    for l in range(0, CMP_LEN, 2):
        pair = jnp.concatenate([token(l), token(l + 1)], axis=1)
        acc = acc + _dot(pair, w1_ref[l * HEAD_DIM:(l + 2) * HEAD_DIM, :])
    hid = jax.nn.gelu(acc)
    o_ref[...] = _dot(hid.astype(BF16), w2_ref[...]).astype(BF16)


def _compress(kv, pe, w1_b, w2_b):
    B, S, _ = kv.shape
    n_blk = S // CMP_STRIDE
    kernel = functools.partial(_compress_kernel, seq=S, n_blk=n_blk)
    return pl.pallas_call(
        kernel,
        grid=(B, N_KV_HEADS),
        in_specs=[
            pl.BlockSpec((None, S, HEAD_DIM), lambda b, g: (b, 0, g)),
            pl.BlockSpec((CMP_LEN, HEAD_DIM), lambda b, g: (0, 0)),
            pl.BlockSpec((CMP_LEN * HEAD_DIM, CMP_HIDDEN), lambda b, g: (0, 0)),
            pl.BlockSpec((CMP_HIDDEN, HEAD_DIM), lambda b, g: (0, 0)),
        ],
        out_specs=pl.BlockSpec((None, None, n_blk, HEAD_DIM), lambda b, g: (b, g, 0, 0)),
        out_shape=jax.ShapeDtypeStruct((B, N_KV_HEADS, n_blk, HEAD_DIM), BF16),
        scratch_shapes=[pltpu.VMEM((S + CMP_STRIDE, HEAD_DIM), F32)],
        compiler_params=_params(("arbitrary", "arbitrary")),
        name="compress",
    )(kv, pe, w1_b, w2_b)


def _stack_heads(q_ref, row0, tq):
    return jnp.concatenate([q_ref[row0:row0 + tq, r * HEAD_DIM:(r + 1) * HEAD_DIM] for r in range(GQA_GROUP)],
                           axis=0)


def _normalized(acc):
    l = acc[:, HEAD_DIM:2 * HEAD_DIM]
    return acc[:, 0:HEAD_DIM] / jnp.where(l > 0.0, l, 1.0)


def _gate_col(gates, g, r, branch):
    lo = r * N_BRANCH + branch
    hi = (GQA_GROUP + r) * N_BRANCH + branch
    return jnp.where(g == 0, gates[:, lo:lo + 1], gates[:, hi:hi + 1])


def _cmp_scores(q_band, kc_ref, cbias_ref, qs, *, n_blk):
    start = pl.multiple_of(n_blk - qs // CMP_STRIDE, SUBLANES)
    vis_bias = cbias_ref[pl.ds(start, n_blk), :].astype(BF16)
    return _dot_nt(q_band, jnp.concatenate([kc_ref[...], vis_bias], axis=1))


def _cmp_attend(s, vc_ref, s2c_ref, qs, *, tq, n_blk):
    rows = GQA_GROUP * tq
    p = jnp.exp2(s - jnp.max(s, axis=-1, keepdims=True)).astype(BF16)
    ov = _dot(p, jnp.concatenate([vc_ref[...], jnp.ones((n_blk, HEAD_DIM), BF16)], axis=1))
    t_row = qs + (lax.broadcasted_iota(jnp.int32, (rows, HEAD_DIM), 0) & (tq - 1))
    o_cmp = jnp.where(t_row >= CMP_LEN - 1, _normalized(ov), 0.0)

    imp = jnp.zeros((LANES, tq), F32)
    for r in range(GQA_GROUP):
        it = _dot_nt(s2c_ref[...], p[r * tq:(r + 1) * tq, :])
        l_t = it[LANES:LANES + 1, :]
        imp = imp + it[0:LANES, :] / jnp.where(l_t > 0.0, l_t, 1.0)
    t = qs + lax.broadcasted_iota(jnp.int32, (LANES, tq), 1)
    imp = jnp.where(t >= CMP_LEN - 1, imp, 0.0)
    blk = lax.broadcasted_iota(jnp.int32, (LANES, tq), 0)
    cur = t // SLC_LEN
    valid = blk * SLC_LEN <= t
    forced = (blk == 0) | (blk == cur) | (blk == cur - 1)
    return o_cmp, jnp.where(valid, imp + jnp.where(forced, FORCE_BONUS, 0.0), -1.0), valid


def _cmp_select(score, valid, tri_ref, eye_ref, qs, *, tq):
    x = score
    covered = jnp.zeros((1, tq), F32)
    n_before = jnp.zeros((1, tq), F32)
    theta = jnp.full((1, tq), -1.0, F32)
    for _ in range(N_SLC):
        mx = jnp.max(x, axis=0, keepdims=True)
        eq = x == mx
        cnt = jnp.sum(jnp.where(eq, 1.0, 0.0), axis=0, keepdims=True)
        crossing = (covered < float(N_SLC)) & (covered + cnt >= float(N_SLC))
        theta = jnp.where(crossing, mx, theta)
        n_before = jnp.where(crossing, covered, n_before)
        covered = covered + cnt
        x = jnp.where(eq, -jnp.inf, x)
    tie = score == theta
    tie_rank = _dot(tri_ref[...], jnp.where(tie, 1.0, 0.0).astype(BF16))
    chosen = (score > theta) | (tie & (tie_rank <= float(N_SLC) - n_before))
    sel_t = jnp.where(valid & chosen, 1.0, 0.0).astype(BF16)
    sel = _dot_nt(eye_ref[...], sel_t)
    below = lax.broadcasted_iota(jnp.int32, (tq, LANES), 1) < qs // SLC_LEN
    return jnp.where(below, (sel - 1.0) * (-MASK_BIAS), MASK_BIAS).astype(BF16)


def _nsa_kernel(q_ref, qn_ref, kc_ref, vc_ref, cbias_ref, s2c_ref, tri_ref, eye_ref, ksl_ref, vsl_ref, kwn_ref,
                vwn_ref, onehot_ref, wbias_ref, gates_in_ref, y_ref, s_ref, smax_ref, ocmp_ref, selb_ref, gate_ref,
                *, tq, tk, n_blk):
    g = pl.program_id(1)
    step = pl.program_id(2)
    rows = GQA_GROUP * tq
    row_id = lax.broadcasted_iota(jnp.int32, (rows, tq), 0) & (tq - 1)
    eye = jnp.where(row_id == lax.broadcasted_iota(jnp.int32, (rows, tq), 1), 1.0, 0.0).astype(BF16)
    ones_k = jnp.ones((tk, HEAD_DIM), BF16)
    ones_q = jnp.ones((tq, HEAD_DIM), BF16)
    for h in range(TILES_PER_STEP):
        _nsa_tile(h, step, g, eye, ones_k, ones_q, q_ref, qn_ref, kc_ref, vc_ref, cbias_ref, s2c_ref, tri_ref, eye_ref,
                  ksl_ref, vsl_ref, kwn_ref, vwn_ref, onehot_ref, wbias_ref, gates_in_ref, y_ref, s_ref, smax_ref,
                  ocmp_ref, selb_ref, gate_ref, tq=tq, tk=tk, n_blk=n_blk)


def _nsa_tile(h, step, g, eye, ones_k, ones_q, q_ref, qn_ref, kc_ref, vc_ref, cbias_ref, s2c_ref, tri_ref, eye_ref,
              ksl_ref, vsl_ref, kwn_ref, vwn_ref, onehot_ref, wbias_ref, gates_in_ref, y_ref, s_ref, smax_ref,
              ocmp_ref, selb_ref, gate_ref, *, tq, tk, n_blk):
    i = step * TILES_PER_STEP + h
    qs = i * tq
    slot = h
    r0 = h * tq
    rows = GQA_GROUP * tq

    if h == 0:
        @pl.when(step == 0)
        def _():
            q0 = jnp.concatenate([_stack_heads(q_ref, 0, tq), eye], axis=1)
            s0 = _cmp_scores(q0, kc_ref, cbias_ref, qs, n_blk=n_blk)
            ocmp_ref[0], score0, valid0 = _cmp_attend(s0, vc_ref, s2c_ref, qs, tq=tq, n_blk=n_blk)
            selb_ref[0] = _cmp_select(score0, valid0, tri_ref, eye_ref, qs, tq=tq)

    qst = _stack_heads(q_ref, r0, tq)
    q_band = jnp.concatenate([qst, eye], axis=1)
    sel_bias = selb_ref[slot]
    gates = gates_in_ref[r0:r0 + tq, :]
    for r in range(GQA_GROUP):
        for br in range(N_BRANCH):
            gate_ref[r * N_BRANCH + br] = jnp.broadcast_to(_gate_col(gates, g, r, br), (tq, HEAD_DIM))

    qs_next = qs + tq
    q_next = jnp.concatenate([_stack_heads(q_ref, tq, tq) if h == 0 else _stack_heads(qn_ref, 0, tq), eye], axis=1)
    s_next = _cmp_scores(q_next, kc_ref, cbias_ref, qs_next, n_blk=n_blk)

    n_chunk = WINDOW // tq + 1
    k_parts, v_parts = [], []
    for c in range(n_chunk):
        start = qs - WINDOW + c * tq
        src = pl.multiple_of(jnp.maximum(start, 0), tq)
        bias_c = wbias_ref[c * tq:(c + 1) * tq, :]
        bias_c = jnp.where(start >= 0, bias_c, jnp.full(bias_c.shape, MASK_BIAS, BF16))
        k_parts.append(jnp.concatenate([kwn_ref[pl.ds(src, tq), :], bias_c], axis=1))
        v_parts.append(jnp.concatenate([vwn_ref[pl.ds(src, tq), :], ones_q], axis=1))
    own = pl.ds(pl.multiple_of(qs, tq), tq)
    causal_bias = wbias_ref[(n_chunk - 1) * tq:n_chunk * tq, :]
    k_parts.append(jnp.concatenate([ksl_ref[own, :], causal_bias], axis=1))
    s_band = _dot_nt(q_band, jnp.concatenate(k_parts, axis=0))

    ocmp_ref[1 - slot], score_next, valid_next = _cmp_attend(s_next, vc_ref, s2c_ref, qs_next, tq=tq, n_blk=n_blk)

    s_w = s_band[:, 0:WINDOW + tq]
    p_w = jnp.exp2(s_w - jnp.max(s_w, axis=-1, keepdims=True))
    acc_w = _dot(p_w.astype(BF16), jnp.concatenate(v_parts, axis=0))
    o_win = _normalized(acc_w)
    s_own = s_band[:, WINDOW + tq:WINDOW + 2 * tq]
    m_own = jnp.max(s_own, axis=-1, keepdims=True)
    acc_own = _dot(jnp.exp2(s_own - m_own).astype(BF16), jnp.concatenate([vsl_ref[own, :], ones_q], axis=1))

    q_slc = jnp.concatenate([qst, jnp.concatenate([sel_bias] * GQA_GROUP, axis=0)], axis=1)

    def put_scores(kt, slot):
        k0 = pl.multiple_of(kt * tk, tk)
        k_aug = jnp.concatenate([ksl_ref[pl.ds(k0, tk), :], onehot_ref[pl.ds(k0, tk), :]], axis=1)
        s = _dot_nt(q_slc, k_aug)
        s_ref[slot] = s
        smax_ref[slot] = jnp.broadcast_to(jnp.max(s, axis=-1, keepdims=True), (rows, LANES))

    put_scores(0, 0)

    selb_ref[1 - slot] = _cmp_select(score_next, valid_next, tri_ref, eye_ref, qs_next, tq=tq)

    def update(kt, s, s_max, carry):
        m, acc = carry
        k0 = pl.multiple_of(kt * tk, tk)
        m_new = jnp.maximum(m, s_max)
        p = jnp.exp2(s - jnp.concatenate([m_new] * (tk // LANES), axis=1))
        alpha = jnp.concatenate([jnp.exp2(m - m_new)] * 2, axis=1)
        v_aug = jnp.concatenate([vsl_ref[pl.ds(k0, tk), :], ones_k], axis=1)
        return m_new, alpha * acc + _dot(p.astype(BF16), v_aug)

    n_kt = jnp.maximum((qs + tk - 1) // tk, 1)

    def body(j, carry):
        kt = 2 * j
        s_cur, s_max = s_ref[0], smax_ref[0]
        put_scores(kt + 1, 1)
        carry = update(kt, s_cur, s_max, carry)
        s_cur, s_max = s_ref[1], smax_ref[1]
        put_scores(jnp.minimum(kt + 2, n_kt - 1), 0)
        return update(kt + 1, s_cur, s_max, carry)

    m_init = jnp.broadcast_to(m_own, (rows, LANES))
    carry = lax.fori_loop(0, n_kt // 2, body, (m_init, acc_own))
    _, acc_s = lax.cond(n_kt % 2 == 1,
                        lambda c: update(n_kt - 1, s_ref[0], smax_ref[0], c),
                        lambda c: c, carry)
    o_slc = _normalized(acc_s)

    o_cmp = ocmp_ref[slot]
    for r in range(GQA_GROUP):
        rs = slice(r * tq, (r + 1) * tq)
        cs = slice(r * HEAD_DIM, (r + 1) * HEAD_DIM)
        y_ref[r0:r0 + tq, cs] = (gate_ref[r * N_BRANCH] * o_cmp[rs, :] + gate_ref[r * N_BRANCH + 1] * o_slc[rs, :]
                        + gate_ref[r * N_BRANCH + 2] * o_win[rs, :])


def _nsa_attention(q, k_cmp, v_cmp, ksl, vsl, kwn, vwn, cbias, s2c, tri, eye, onehot, wbias, gates):
    B, S, _ = q.shape
    n_blk = k_cmp.shape[2]
    tq = min(TQ, S)
    tk = min(TK_SLC, S)
    n_q = S // (TILES_PER_STEP * tq)
    tb = TILES_PER_STEP * tq
    grp = GQA_GROUP * HEAD_DIM
    kernel = functools.partial(_nsa_kernel, tq=tq, tk=tk, n_blk=n_blk)
    kv = pl.BlockSpec((None, S, HEAD_DIM), lambda b, g, i: (b, 0, g))
    cmp_kv = pl.BlockSpec((None, None, n_blk, HEAD_DIM), lambda b, g, i: (b, g, 0, 0))
    const = lambda shape: pl.BlockSpec(shape, lambda b, g, i: (0, 0))
    return pl.pallas_call(
        kernel,
        grid=(B, N_KV_HEADS, n_q),
        in_specs=[
            pl.BlockSpec((None, tb, grp), lambda b, g, i: (b, i, g)),
            pl.BlockSpec((None, tb, grp), lambda b, g, i: (b, jnp.minimum(i + 1, n_q - 1), g)),
            cmp_kv, cmp_kv, const(cbias.shape), const(s2c.shape), const((LANES, LANES)), const((tq, tq)),
            kv, kv, kv, kv,
            const((S, LANES)), const((WINDOW + tq, tq)),
            pl.BlockSpec((None, tb, LANES), lambda b, g, i: (b, i, 0)),
        ],
        out_specs=pl.BlockSpec((None, tb, grp), lambda b, g, i: (b, i, g)),
        out_shape=jax.ShapeDtypeStruct((B, S, D_ATTN), F32),
        scratch_shapes=[pltpu.VMEM((2, GQA_GROUP * tq, tk), F32),
                        pltpu.VMEM((2, GQA_GROUP * tq, LANES), F32),
                        pltpu.VMEM((2, GQA_GROUP * tq, HEAD_DIM), F32),
                        pltpu.VMEM((2, tq, LANES), BF16),
                        pltpu.VMEM((GQA_GROUP * N_BRANCH, tq, HEAD_DIM), F32)],
        compiler_params=_params(("arbitrary", "arbitrary", "arbitrary")),
        name="nsa_attention",
    )(q, q, k_cmp, v_cmp, cbias, s2c, tri, eye, ksl, vsl, kwn, vwn, onehot, wbias, gates)


def _out_proj_kernel(yc_ref, ya_ref, x_ref, w_ref, ga_ref, g1_ref, n2_ref, sh_ref, sc_ref, x1_ref, h2_ref):
    ya = _rms(ya_ref[...], ga_ref[...]).astype(BF16)
    mix = _dot(yc_ref[...], w_ref[0:D_CONV, :]) + _dot(ya, w_ref[D_CONV:D_CONV + D_ATTN, :])
    x1 = x_ref[...] + g1_ref[...] * mix
    x1_ref[...] = x1
    h2_ref[...] = (_rms(x1, n2_ref[...]) * (1.0 + sc_ref[...]) + sh_ref[...]).astype(BF16)


def _out_proj(yconv, yattn, x, w_out_b, gattn_g, mod, norm2_g):
    B, S, D = x.shape
    tm = min(TM_PROJ, S)
    tok = lambda width: pl.BlockSpec((None, tm, width), lambda b, i: (b, i, 0))
    vec = lambda width: pl.BlockSpec((1, width), lambda b, i: (0, 0))
    modspec = lambda k: pl.BlockSpec((None, None, 1, D), lambda b, i, k=k: (b, k, 0, 0))
    return pl.pallas_call(
        _out_proj_kernel,
        grid=(B, S // tm),
        in_specs=[tok(D_CONV), tok(D_ATTN), tok(D),
                  pl.BlockSpec((D_CONV + D_ATTN, D), lambda b, i: (0, 0), pipeline_mode=pl.Buffered(1)),
                  vec(D_ATTN), modspec(2), vec(D), modspec(3), modspec(4)],
        out_specs=[tok(D), tok(D)],
        out_shape=[jax.ShapeDtypeStruct((B, S, D), F32), jax.ShapeDtypeStruct((B, S, D), BF16)],
        compiler_params=_params(("arbitrary", "arbitrary")),
        name="out_proj",
    )(yconv, yattn, x, w_out_b, gattn_g, mod, norm2_g, mod, mod)


def _ffn_kernel(h_ref, w1_ref, w2_ref, x1_ref, g2_ref, nf_ref, o_ref, *, final_norm):
    f = pl.program_id(2)

    @pl.when(f == 0)
    def _():
        o_ref[...] = jnp.zeros(o_ref.shape, F32)

    a = jnp.maximum(_dot(h_ref[...], w1_ref[...]), 0.0)
    o_ref[...] += _dot((a * a).astype(BF16), w2_ref[...])

    @pl.when(f == pl.num_programs(2) - 1)
    def _():
        x2 = x1_ref[...] + g2_ref[...] * o_ref[...]
        o_ref[...] = _rms(x2, nf_ref[...]) if final_norm else x2


def _ffn(h2, w1_b, w2_b, x1, mod, normf_g, final_norm):
    B, S, D = x1.shape
    d_ff = w1_b.shape[1]
    tm = min(TM_FFN, S)
    tf = min(TF_FFN, d_ff)
    tok = pl.BlockSpec((None, tm, D), lambda b, i, f: (b, i, 0))
    return pl.pallas_call(
        functools.partial(_ffn_kernel, final_norm=final_norm),
        grid=(B, S // tm, d_ff // tf),
        in_specs=[tok,
                  pl.BlockSpec((D, tf), lambda b, i, f: (0, f)),
                  pl.BlockSpec((tf, D), lambda b, i, f: (f, 0)),
                  tok,
                  pl.BlockSpec((None, None, 1, D), lambda b, i, f: (b, 5, 0, 0)),
                  pl.BlockSpec((1, D), lambda b, i, f: (0, 0))],
        out_specs=tok,
        out_shape=jax.ShapeDtypeStruct((B, S, D), F32),
        compiler_params=_params(("arbitrary", "arbitrary", "arbitrary"), FFN_VMEM_LIMIT_BYTES),
        name="ffn",
    )(h2, w1_b, w2_b, x1, mod, normf_g)


def _rope_tables(seq):
    inv = ROPE_THETA ** (-jnp.arange(0, HEAD_DIM, 2, dtype=F32) / HEAD_DIM)
    ang = jnp.arange(seq, dtype=F32)[:, None] * inv[None, :]
    cos, sin = jnp.cos(ang), jnp.sin(ang)
    return jnp.concatenate([cos, cos], axis=-1), jnp.concatenate([-sin, sin], axis=-1)


def _slc_from_cmp(n_blk):
    sj = np.arange(LANES)[:, None]
    ci = np.arange(n_blk)[None, :]
    m = (ci * CMP_STRIDE <= sj * SLC_LEN + SLC_LEN - 1) & (ci * CMP_STRIDE + CMP_LEN - 1 >= sj * SLC_LEN)
    return jnp.asarray(np.concatenate([m, np.ones((2 * SUBLANES, n_blk), bool)], axis=0), dtype=BF16)


def _cmp_visibility_bias(n_blk, tq):
    d = np.arange(2 * n_blk)[:, None] - n_blk
    r = np.arange(tq)[None, :]
    return jnp.asarray(np.where(CMP_STRIDE * d + CMP_LEN - 1 <= r, 0.0, MASK_BIAS), dtype=F32)


def _block_onehot(seq):
    m = (np.arange(seq)[:, None] // SLC_LEN) == np.arange(LANES)[None, :]
    return jnp.asarray(m, dtype=BF16)


def _prefix_ones():
    return jnp.asarray(np.arange(LANES)[:, None] >= np.arange(LANES)[None, :], dtype=BF16)


def _window_bias(tq):
    key = np.arange(WINDOW + tq)[:, None]
    row = np.arange(tq)[None, :]
    ok = (key > row) & (key <= row + WINDOW)
    return jnp.asarray(np.where(ok, 0.0, MASK_BIAS), dtype=BF16)


def kernel(x, c, w_ada, b_ada, norm1_g, w_in, conv_w, conv_b, cmp_pe_k, cmp_pe_v, cmp_w1_k, cmp_w2_k,
           cmp_w1_v, cmp_w2_v, gnorm_conv_g, gnorm_attn_g, w_out, norm2_g, w_ff1, w_ff2, normf_g):
    B, S, D = x.shape
    depth = w_ada.shape[0]
    assert S % (TILES_PER_STEP * TQ) == 0 and S // SLC_LEN <= LANES and S >= WINDOW + TQ
    assert w_in.shape[2] == D_IN
    cos_f, sin_f = _rope_tables(S)
    s2c = _slc_from_cmp(S // CMP_STRIDE)
    cbias = _cmp_visibility_bias(S // CMP_STRIDE, min(TQ, S))
    onehot = _block_onehot(S)
    tri = _prefix_ones()
    eye = jnp.eye(min(TQ, S), dtype=BF16)
    wbias = _window_bias(min(TQ, S))
    for l in range(depth):
        mod = _adaln(c, w_ada[l], b_ada[l]).reshape(B, 6, 1, D)
        w_in_b = w_in[l].astype(BF16)
        w_gate_b = jnp.pad(w_in[l][:, COL_GATE:], ((0, 0), (0, LANES - N_BRANCH * N_HEADS))).astype(BF16)
        (yconv, q, kc, vc, ksl, vsl, kwn, vwn, gates) = _in_proj(
            x, mod, norm1_g[l][None], w_in_b, w_gate_b, conv_w[l], conv_b[l][None], gnorm_conv_g[l][None], cos_f, sin_f)
        k_cmp = _compress(kc, cmp_pe_k[l], cmp_w1_k[l].astype(BF16), cmp_w2_k[l].astype(BF16))
        v_cmp = _compress(vc, cmp_pe_v[l], cmp_w1_v[l].astype(BF16), cmp_w2_v[l].astype(BF16))
        yattn = _nsa_attention(q, k_cmp, v_cmp, ksl, vsl, kwn, vwn, cbias, s2c, tri, eye, onehot, wbias, gates)
        x1, h2 = _out_proj(yconv, yattn, x, w_out[l].astype(BF16), gnorm_attn_g[l][None], mod, norm2_g[l][None])
        x = _ffn(h2, w_ff1[l].astype(BF16), w_ff2[l].astype(BF16), x1, mod, normf_g[None],
                 final_norm=(l == depth - 1))
    return x
```

```python
import functools

import numpy as np
import jax
import jax.numpy as jnp
from jax import lax
from jax.experimental import pallas as pl
from jax.experimental.pallas import tpu as pltpu

F32 = jnp.float32
BF16 = jnp.bfloat16

HEAD_DIM = 128
N_HEADS = 8
N_KV_HEADS = 2
GQA_GROUP = N_HEADS // N_KV_HEADS
D_CONV = 1024
D_ATTN = N_HEADS * HEAD_DIM
D_KV = N_KV_HEADS * HEAD_DIM
N_BRANCH = 3
CONV_WIDTH = 3
CMP_LEN = 32
CMP_STRIDE = 16
CMP_HIDDEN = 256
SLC_LEN = 64
N_SLC = 16
WINDOW = 512
ROPE_THETA = 10000.0
EPS = 1e-6
FORCE_BONUS = 1e4

LANES = 128
SUBLANES = 8
VMEM_LIMIT_BYTES = 56 * 1024 * 1024
FFN_VMEM_LIMIT_BYTES = 61 * 1024 * 1024

NEG = float(np.finfo(np.float32).min)
MASK_BIAS = -(2.0 ** 126)
LOG2E = float(np.log2(np.e))

COL_UB, COL_UC, COL_UH = 0, D_CONV, 2 * D_CONV
COL_Q = 3 * D_CONV
COL_KV = COL_Q + D_ATTN
COL_GATE = COL_KV + 2 * N_BRANCH * D_KV
D_IN = COL_GATE + N_BRANCH * N_HEADS

TM_PROJ = 512
TQ = 128
TILES_PER_STEP = 4
TK_SLC = 1024
TM_FFN = 512
TF_FFN = 2048
TN_ADA = 1024


def _params(sem, vmem_limit_bytes=VMEM_LIMIT_BYTES):
    return pltpu.CompilerParams(dimension_semantics=sem, vmem_limit_bytes=vmem_limit_bytes)


def _dot(a, b):
    return jnp.dot(a, b, preferred_element_type=F32)


def _dot_nt(a, b):
    return lax.dot_general(a, b, (((1,), (1,)), ((), ())), preferred_element_type=F32)


def _rms(x, g):
    return x * lax.rsqrt(jnp.mean(x * x, axis=-1, keepdims=True) + EPS) * g


def _adaln_kernel(c_ref, w_ref, b_ref, o_ref):
    c = c_ref[...]
    s = c * jax.nn.sigmoid(c)
    o_ref[...] = _dot(s.astype(BF16), w_ref[...].astype(BF16)) + b_ref[...]


def _adaln(c, w_ada, b_ada):
    B, D = c.shape
    n_out = w_ada.shape[1]
    rows = -(-B // SUBLANES) * SUBLANES
    c_pad = jnp.zeros((rows, D), F32).at[:B].set(c)
    out = pl.pallas_call(
        _adaln_kernel,
        grid=(n_out // TN_ADA,),
        in_specs=[
            pl.BlockSpec((rows, D), lambda j: (0, 0)),
            pl.BlockSpec((D, TN_ADA), lambda j: (0, j)),
            pl.BlockSpec((1, TN_ADA), lambda j: (0, j)),
        ],
        out_specs=pl.BlockSpec((rows, TN_ADA), lambda j: (0, j)),
        out_shape=jax.ShapeDtypeStruct((rows, n_out), F32),
        compiler_params=_params(("arbitrary",)),
        name="adaln",
    )(c_pad, w_ada, b_ada.reshape(1, n_out))
    return out[:B]


def _rope(u, cos_f, sin_f):
    return u * cos_f + pltpu.roll(u, HEAD_DIM // 2, 1) * sin_f


def _in_proj_kernel(x_ref, sh_ref, sc_ref, g_ref, w_ref, wg_ref, cw_ref, cb_ref, gc_ref, cos_ref, sin_ref,
                    yconv_ref, q_ref, kc_ref, vc_ref, ksl_ref, vsl_ref, kwn_ref, vwn_ref, gate_ref,
                    vbuf_ref, *, tm, q_scale):
    @pl.when(pl.program_id(1) == 0)
    def _():
        vbuf_ref[tm:tm + SUBLANES, :] = jnp.zeros((SUBLANES, D_CONV), F32)

    x = x_ref[...]
    h = _rms(x, g_ref[...]) * (1.0 + sc_ref[...]) + sh_ref[...]
    hb = h.astype(BF16)

    ub = _dot(hb, w_ref[:, COL_UB:COL_UB + D_CONV])
    uc = _dot(hb, w_ref[:, COL_UC:COL_UC + D_CONV])
    uh = _dot(hb, w_ref[:, COL_UH:COL_UH + D_CONV])
    v = uc * uh
    vbuf_ref[0:SUBLANES, :] = vbuf_ref[tm:tm + SUBLANES, :]
    vbuf_ref[SUBLANES:SUBLANES + tm, :] = v
    v1 = vbuf_ref[SUBLANES - 1:SUBLANES - 1 + tm, :]
    v2 = vbuf_ref[SUBLANES - 2:SUBLANES - 2 + tm, :]
    z = cb_ref[...] + cw_ref[0:1, :] * v2 + cw_ref[1:2, :] * v1 + cw_ref[2:3, :] * v
    yconv_ref[...] = _rms(ub * z, gc_ref[...]).astype(BF16)

    cos_f = cos_ref[...]
    sin_f = sin_ref[...]

    uq = _dot(hb, w_ref[:, COL_Q:COL_Q + D_ATTN])
    for hd in range(N_HEADS):
        sl = slice(hd * HEAD_DIM, (hd + 1) * HEAD_DIM)
        q_ref[:, sl] = (_rope(uq[:, sl], cos_f, sin_f) * q_scale).astype(BF16)

    ug = _dot(hb, wg_ref[...])
    gate_ref[...] = jax.nn.sigmoid(ug)

    outs = (kc_ref, vc_ref, ksl_ref, vsl_ref, kwn_ref, vwn_ref)
    for n in (0, 2, 4, 1, 3, 5):
        u = _dot(hb, w_ref[:, COL_KV + n * D_KV:COL_KV + (n + 1) * D_KV])
        for g in range(N_KV_HEADS):
            cs = slice(g * HEAD_DIM, (g + 1) * HEAD_DIM)
            t = _rope(u[:, cs], cos_f, sin_f) if n % 2 == 0 else u[:, cs]
            outs[n][:, cs] = t.astype(outs[n].dtype)


def _in_proj(x, mod, norm1_g, w_in_b, w_gate_b, conv_w, conv_b, gconv_g, cos_f, sin_f):
    B, S, D = x.shape
    tm = min(TM_PROJ, S)
    tok = lambda width: pl.BlockSpec((None, tm, width), lambda b, i: (b, i, 0))
    vec = lambda width: pl.BlockSpec((1, width), lambda b, i: (0, 0))
    modspec = lambda k: pl.BlockSpec((None, None, 1, D), lambda b, i, k=k: (b, k, 0, 0))
    kernel = functools.partial(_in_proj_kernel, tm=tm, q_scale=LOG2E * HEAD_DIM ** -0.5)
    sd = jax.ShapeDtypeStruct
    return pl.pallas_call(
        kernel,
        grid=(B, S // tm),
        in_specs=[
            tok(D), modspec(0), modspec(1), vec(D),
            pl.BlockSpec((D, D_IN), lambda b, i: (0, 0), pipeline_mode=pl.Buffered(1)),
            pl.BlockSpec((D, LANES), lambda b, i: (0, 0)),
            pl.BlockSpec((CONV_WIDTH, D_CONV), lambda b, i: (0, 0)), vec(D_CONV), vec(D_CONV),
            pl.BlockSpec((tm, HEAD_DIM), lambda b, i: (i, 0)),
            pl.BlockSpec((tm, HEAD_DIM), lambda b, i: (i, 0)),
        ],
        out_specs=[tok(D_CONV), tok(D_ATTN), tok(D_KV), tok(D_KV), tok(D_KV), tok(D_KV), tok(D_KV),
                   tok(D_KV), tok(LANES)],
        out_shape=[sd((B, S, D_CONV), BF16), sd((B, S, D_ATTN), BF16),
                   sd((B, S, D_KV), F32), sd((B, S, D_KV), F32),
                   sd((B, S, D_KV), BF16), sd((B, S, D_KV), BF16),
                   sd((B, S, D_KV), BF16), sd((B, S, D_KV), BF16),
                   sd((B, S, LANES), F32)],
        scratch_shapes=[pltpu.VMEM((tm + 2 * SUBLANES, D_CONV), F32)],
        compiler_params=_params(("arbitrary", "arbitrary")),
        name="in_proj",
    )(x, mod, mod, norm1_g, w_in_b, w_gate_b, conv_w, conv_b, gconv_g, cos_f, sin_f)


def _compress_kernel(kv_ref, pe_ref, w1_ref, w2_ref, o_ref, buf_ref, *, seq, n_blk):
    buf_ref[0:seq, :] = kv_ref[...]
    buf_ref[seq:seq + CMP_STRIDE, :] = jnp.zeros((CMP_STRIDE, HEAD_DIM), F32)
    def token(l):
        return (buf_ref[pl.ds(l, n_blk, stride=CMP_STRIDE), :] + pe_ref[l:l + 1, :]).astype(BF16)

    acc = jnp.zeros((n_blk, CMP_HIDDEN), F32)
    for l in range(0, CMP_LEN, 2):
        pair = jnp.concatenate([token(l), token(l + 1)], axis=1)
        acc = acc + _dot(pair, w1_ref[l * HEAD_DIM:(l + 2) * HEAD_DIM, :])
    hid = jax.nn.gelu(acc)
    o_ref[...] = _dot(hid.astype(BF16), w2_ref[...]).astype(BF16)


def _compress(kv, pe, w1_b, w2_b):
    B, S, _ = kv.shape
    n_blk = S // CMP_STRIDE
    kernel = functools.partial(_compress_kernel, seq=S, n_blk=n_blk)
    return pl.pallas_call(
        kernel,
        grid=(B, N_KV_HEADS),
        in_specs=[
            pl.BlockSpec((None, S, HEAD_DIM), lambda b, g: (b, 0, g)),
            pl.BlockSpec((CMP_LEN, HEAD_DIM), lambda b, g: (0, 0)),
            pl.BlockSpec((CMP_LEN * HEAD_DIM, CMP_HIDDEN), lambda b, g: (0, 0)),
            pl.BlockSpec((CMP_HIDDEN, HEAD_DIM), lambda b, g: (0, 0)),
        ],
        out_specs=pl.BlockSpec((None, None, n_blk, HEAD_DIM), lambda b, g: (b, g, 0, 0)),
        out_shape=jax.ShapeDtypeStruct((B, N_KV_HEADS, n_blk, HEAD_DIM), BF16),
        scratch_shapes=[pltpu.VMEM((S + CMP_STRIDE, HEAD_DIM), F32)],
        compiler_params=_params(("arbitrary", "arbitrary")),
        name="compress",
    )(kv, pe, w1_b, w2_b)


def _stack_heads(q_ref, row0, tq):
    return jnp.concatenate([q_ref[row0:row0 + tq, r * HEAD_DIM:(r + 1) * HEAD_DIM] for r in range(GQA_GROUP)],
                           axis=0)


def _normalized(acc):
    l = acc[:, HEAD_DIM:2 * HEAD_DIM]
    return acc[:, 0:HEAD_DIM] / jnp.where(l > 0.0, l, 1.0)


def _gate_col(gates, g, r, branch):
    lo = r * N_BRANCH + branch
    hi = (GQA_GROUP + r) * N_BRANCH + branch
    return jnp.where(g == 0, gates[:, lo:lo + 1], gates[:, hi:hi + 1])


def _cmp_scores(q_band, kc_ref, cbias_ref, qs, *, n_blk):
    start = pl.multiple_of(n_blk - qs // CMP_STRIDE, SUBLANES)
    vis_bias = cbias_ref[pl.ds(start, n_blk), :].astype(BF16)
    return _dot_nt(q_band, jnp.concatenate([kc_ref[...], vis_bias], axis=1))


def _cmp_attend(s, vc_ref, s2c_ref, qs, *, tq, n_blk):
    rows = GQA_GROUP * tq
    p = jnp.exp2(s - jnp.max(s, axis=-1, keepdims=True)).astype(BF16)
    ov = _dot(p, jnp.concatenate([vc_ref[...], jnp.ones((n_blk, HEAD_DIM), BF16)], axis=1))
    t_row = qs + (lax.broadcasted_iota(jnp.int32, (rows, HEAD_DIM), 0) & (tq - 1))
    o_cmp = jnp.where(t_row >= CMP_LEN - 1, _normalized(ov), 0.0)

    imp = jnp.zeros((LANES, tq), F32)
    for r in range(GQA_GROUP):
        it = _dot_nt(s2c_ref[...], p[r * tq:(r + 1) * tq, :])
        l_t = it[LANES:LANES + 1, :]
        imp = imp + it[0:LANES, :] / jnp.where(l_t > 0.0, l_t, 1.0)
    t = qs + lax.broadcasted_iota(jnp.int32, (LANES, tq), 1)
    imp = jnp.where(t >= CMP_LEN - 1, imp, 0.0)
    blk = lax.broadcasted_iota(jnp.int32, (LANES, tq), 0)
    cur = t // SLC_LEN
    valid = blk * SLC_LEN <= t
    forced = (blk == 0) | (blk == cur) | (blk == cur - 1)
    return o_cmp, jnp.where(valid, imp + jnp.where(forced, FORCE_BONUS, 0.0), -1.0), valid


def _cmp_select(score, valid, tri_ref, eye_ref, qs, *, tq):
    x = score
    covered = jnp.zeros((1, tq), F32)
    n_before = jnp.zeros((1, tq), F32)
    theta = jnp.full((1, tq), -1.0, F32)
    for _ in range(N_SLC):
        mx = jnp.max(x, axis=0, keepdims=True)
        eq = x == mx
        cnt = jnp.sum(jnp.where(eq, 1.0, 0.0), axis=0, keepdims=True)
        crossing = (covered < float(N_SLC)) & (covered + cnt >= float(N_SLC))
        theta = jnp.where(crossing, mx, theta)
        n_before = jnp.where(crossing, covered, n_before)
        covered = covered + cnt
        x = jnp.where(eq, -jnp.inf, x)
    tie = score == theta
    tie_rank = _dot(tri_ref[...], jnp.where(tie, 1.0, 0.0).astype(BF16))
    chosen = (score > theta) | (tie & (tie_rank <= float(N_SLC) - n_before))
    sel_t = jnp.where(valid & chosen, 1.0, 0.0).astype(BF16)
    sel = _dot_nt(eye_ref[...], sel_t)
    below = lax.broadcasted_iota(jnp.int32, (tq, LANES), 1) < qs // SLC_LEN
    return jnp.where(below, (sel - 1.0) * (-MASK_BIAS), MASK_BIAS).astype(BF16)


def _nsa_kernel(q_ref, qn_ref, kc_ref, vc_ref, cbias_ref, s2c_ref, tri_ref, eye_ref, ksl_ref, vsl_ref, kwn_ref,
                vwn_ref, onehot_ref, wbias_ref, gates_in_ref, y_ref, s_ref, smax_ref, ocmp_ref, selb_ref, gate_ref,
                *, tq, tk, n_blk):
    g = pl.program_id(1)
    step = pl.program_id(2)
    rows = GQA_GROUP * tq
    row_id = lax.broadcasted_iota(jnp.int32, (rows, tq), 0) & (tq - 1)
    eye = jnp.where(row_id == lax.broadcasted_iota(jnp.int32, (rows, tq), 1), 1.0, 0.0).astype(BF16)
    ones_k = jnp.ones((tk, HEAD_DIM), BF16)
    ones_q = jnp.ones((tq, HEAD_DIM), BF16)
    for h in range(TILES_PER_STEP):
        _nsa_tile(h, step, g, eye, ones_k, ones_q, q_ref, qn_ref, kc_ref, vc_ref, cbias_ref, s2c_ref, tri_ref, eye_ref,
                  ksl_ref, vsl_ref, kwn_ref, vwn_ref, onehot_ref, wbias_ref, gates_in_ref, y_ref, s_ref, smax_ref,
                  ocmp_ref, selb_ref, gate_ref, tq=tq, tk=tk, n_blk=n_blk)


def _nsa_tile(h, step, g, eye, ones_k, ones_q, q_ref, qn_ref, kc_ref, vc_ref, cbias_ref, s2c_ref, tri_ref, eye_ref,
              ksl_ref, vsl_ref, kwn_ref, vwn_ref, onehot_ref, wbias_ref, gates_in_ref, y_ref, s_ref, smax_ref,
              ocmp_ref, selb_ref, gate_ref, *, tq, tk, n_blk):
    i = step * TILES_PER_STEP + h
    qs = i * tq
    slot = h % 2
    r0 = h * tq
    rows = GQA_GROUP * tq

    if h == 0:
        @pl.when(step == 0)
        def _():
            q0 = jnp.concatenate([_stack_heads(q_ref, 0, tq), eye], axis=1)
            s0 = _cmp_scores(q0, kc_ref, cbias_ref, qs, n_blk=n_blk)
            ocmp_ref[0], score0, valid0 = _cmp_attend(s0, vc_ref, s2c_ref, qs, tq=tq, n_blk=n_blk)
            selb_ref[0] = _cmp_select(score0, valid0, tri_ref, eye_ref, qs, tq=tq)

    qst = _stack_heads(q_ref, r0, tq)
    q_band = jnp.concatenate([qst, eye], axis=1)
    sel_bias = selb_ref[slot]
    gates = gates_in_ref[r0:r0 + tq, :]
    for r in range(GQA_GROUP):
        for br in range(N_BRANCH):
            gate_ref[r * N_BRANCH + br] = jnp.broadcast_to(_gate_col(gates, g, r, br), (tq, HEAD_DIM))

    qs_next = qs + tq
    last_in_step = h == TILES_PER_STEP - 1
    q_next = jnp.concatenate(
        [_stack_heads(qn_ref, 0, tq) if last_in_step else _stack_heads(q_ref, r0 + tq, tq), eye], axis=1)
    s_next = _cmp_scores(q_next, kc_ref, cbias_ref, qs_next, n_blk=n_blk)

    n_chunk = WINDOW // tq + 1
    k_parts, v_parts = [], []
    for c in range(n_chunk):
        start = qs - WINDOW + c * tq
        src = pl.multiple_of(jnp.maximum(start, 0), tq)
        bias_c = wbias_ref[c * tq:(c + 1) * tq, :]
        bias_c = jnp.where(start >= 0, bias_c, jnp.full(bias_c.shape, MASK_BIAS, BF16))
        k_parts.append(jnp.concatenate([kwn_ref[pl.ds(src, tq), :], bias_c], axis=1))
        v_parts.append(jnp.concatenate([vwn_ref[pl.ds(src, tq), :], ones_q], axis=1))
    own = pl.ds(pl.multiple_of(qs, tq), tq)
    causal_bias = wbias_ref[(n_chunk - 1) * tq:n_chunk * tq, :]
    k_parts.append(jnp.concatenate([ksl_ref[own, :], causal_bias], axis=1))
    s_band = _dot_nt(q_band, jnp.concatenate(k_parts, axis=0))

    ocmp_ref[1 - slot], score_next, valid_next = _cmp_attend(s_next, vc_ref, s2c_ref, qs_next, tq=tq, n_blk=n_blk)

    s_w = s_band[:, 0:WINDOW + tq]
    p_w = jnp.exp2(s_w - jnp.max(s_w, axis=-1, keepdims=True))
    acc_w = _dot(p_w.astype(BF16), jnp.concatenate(v_parts, axis=0))
    o_win = _normalized(acc_w)
    s_own = s_band[:, WINDOW + tq:WINDOW + 2 * tq]
    m_own = jnp.max(s_own, axis=-1, keepdims=True)
    acc_own = _dot(jnp.exp2(s_own - m_own).astype(BF16), jnp.concatenate([vsl_ref[own, :], ones_q], axis=1))

    q_slc = jnp.concatenate([qst, jnp.concatenate([sel_bias] * GQA_GROUP, axis=0)], axis=1)

    def put_scores(kt, slot):
        k0 = pl.multiple_of(kt * tk, tk)
        k_aug = jnp.concatenate([ksl_ref[pl.ds(k0, tk), :], onehot_ref[pl.ds(k0, tk), :]], axis=1)
        s = _dot_nt(q_slc, k_aug)
        s_ref[slot] = s
        smax_ref[slot] = jnp.broadcast_to(jnp.max(s, axis=-1, keepdims=True), (rows, LANES))

    put_scores(0, 0)

    selb_ref[1 - slot] = _cmp_select(score_next, valid_next, tri_ref, eye_ref, qs_next, tq=tq)

    def update(kt, s, s_max, carry):
        m, acc = carry
        k0 = pl.multiple_of(kt * tk, tk)
        m_new = jnp.maximum(m, s_max)
        p = jnp.exp2(s - jnp.concatenate([m_new] * (tk // LANES), axis=1))
        alpha = jnp.concatenate([jnp.exp2(m - m_new)] * 2, axis=1)
        v_aug = jnp.concatenate([vsl_ref[pl.ds(k0, tk), :], ones_k], axis=1)
        return m_new, alpha * acc + _dot(p.astype(BF16), v_aug)

    n_kt = jnp.maximum((qs + tk - 1) // tk, 1)

    def body(j, carry):
        kt = 2 * j
        s_cur, s_max = s_ref[0], smax_ref[0]
        put_scores(kt + 1, 1)
        carry = update(kt, s_cur, s_max, carry)
        s_cur, s_max = s_ref[1], smax_ref[1]
        put_scores(jnp.minimum(kt + 2, n_kt - 1), 0)
        return update(kt + 1, s_cur, s_max, carry)

    m_init = jnp.broadcast_to(m_own, (rows, LANES))
    carry = lax.fori_loop(0, n_kt // 2, body, (m_init, acc_own))
    _, acc_s = lax.cond(n_kt % 2 == 1,
                        lambda c: update(n_kt - 1, s_ref[0], smax_ref[0], c),
                        lambda c: c, carry)
    o_slc = _normalized(acc_s)

    o_cmp = ocmp_ref[slot]
    for r in range(GQA_GROUP):
        rs = slice(r * tq, (r + 1) * tq)
        cs = slice(r * HEAD_DIM, (r + 1) * HEAD_DIM)
        y_ref[r0:r0 + tq, cs] = (gate_ref[r * N_BRANCH] * o_cmp[rs, :] + gate_ref[r * N_BRANCH + 1] * o_slc[rs, :]
                        + gate_ref[r * N_BRANCH + 2] * o_win[rs, :])


def _nsa_attention(q, k_cmp, v_cmp, ksl, vsl, kwn, vwn, cbias, s2c, tri, eye, onehot, wbias, gates):
    B, S, _ = q.shape
    n_blk = k_cmp.shape[2]
    tq = min(TQ, S)
    tk = min(TK_SLC, S)
    n_q = S // (TILES_PER_STEP * tq)
    tb = TILES_PER_STEP * tq
    grp = GQA_GROUP * HEAD_DIM
    kernel = functools.partial(_nsa_kernel, tq=tq, tk=tk, n_blk=n_blk)
    kv = pl.BlockSpec((None, S, HEAD_DIM), lambda b, g, i: (b, 0, g))
    cmp_kv = pl.BlockSpec((None, None, n_blk, HEAD_DIM), lambda b, g, i: (b, g, 0, 0))
    const = lambda shape: pl.BlockSpec(shape, lambda b, g, i: (0, 0))
    return pl.pallas_call(
        kernel,
        grid=(B, N_KV_HEADS, n_q),
        in_specs=[
            pl.BlockSpec((None, tb, grp), lambda b, g, i: (b, i, g)),
            pl.BlockSpec((None, tb, grp), lambda b, g, i: (b, jnp.minimum(i + 1, n_q - 1), g)),
            cmp_kv, cmp_kv, const(cbias.shape), const(s2c.shape), const((LANES, LANES)), const((tq, tq)),
            kv, kv, kv, kv,
            const((S, LANES)), const((WINDOW + tq, tq)),
            pl.BlockSpec((None, tb, LANES), lambda b, g, i: (b, i, 0)),
        ],
        out_specs=pl.BlockSpec((None, tb, grp), lambda b, g, i: (b, i, g)),
        out_shape=jax.ShapeDtypeStruct((B, S, D_ATTN), F32),
        scratch_shapes=[pltpu.VMEM((2, GQA_GROUP * tq, tk), F32),
                        pltpu.VMEM((2, GQA_GROUP * tq, LANES), F32),
                        pltpu.VMEM((2, GQA_GROUP * tq, HEAD_DIM), F32),
                        pltpu.VMEM((2, tq, LANES), BF16),
                        pltpu.VMEM((GQA_GROUP * N_BRANCH, tq, HEAD_DIM), F32)],
        compiler_params=_params(("arbitrary", "arbitrary", "arbitrary")),
        name="nsa_attention",
    )(q, q, k_cmp, v_cmp, cbias, s2c, tri, eye, ksl, vsl, kwn, vwn, onehot, wbias, gates)


def _out_proj_kernel(yc_ref, ya_ref, x_ref, w_ref, ga_ref, g1_ref, n2_ref, sh_ref, sc_ref, x1_ref, h2_ref):
    ya = _rms(ya_ref[...], ga_ref[...]).astype(BF16)
    mix = _dot(yc_ref[...], w_ref[0:D_CONV, :]) + _dot(ya, w_ref[D_CONV:D_CONV + D_ATTN, :])
    x1 = x_ref[...] + g1_ref[...] * mix
    x1_ref[...] = x1
    h2_ref[...] = (_rms(x1, n2_ref[...]) * (1.0 + sc_ref[...]) + sh_ref[...]).astype(BF16)


def _out_proj(yconv, yattn, x, w_out_b, gattn_g, mod, norm2_g):
    B, S, D = x.shape
    tm = min(TM_PROJ, S)
    tok = lambda width: pl.BlockSpec((None, tm, width), lambda b, i: (b, i, 0))
    vec = lambda width: pl.BlockSpec((1, width), lambda b, i: (0, 0))
    modspec = lambda k: pl.BlockSpec((None, None, 1, D), lambda b, i, k=k: (b, k, 0, 0))
    return pl.pallas_call(
        _out_proj_kernel,
        grid=(B, S // tm),
        in_specs=[tok(D_CONV), tok(D_ATTN), tok(D),
                  pl.BlockSpec((D_CONV + D_ATTN, D), lambda b, i: (0, 0), pipeline_mode=pl.Buffered(1)),
                  vec(D_ATTN), modspec(2), vec(D), modspec(3), modspec(4)],
        out_specs=[tok(D), tok(D)],
        out_shape=[jax.ShapeDtypeStruct((B, S, D), F32), jax.ShapeDtypeStruct((B, S, D), BF16)],
        compiler_params=_params(("arbitrary", "arbitrary")),
        name="out_proj",
    )(yconv, yattn, x, w_out_b, gattn_g, mod, norm2_g, mod, mod)


def _ffn_kernel(h_ref, w1_ref, w2_ref, x1_ref, g2_ref, nf_ref, o_ref, *, final_norm):
    f = pl.program_id(2)

    @pl.when(f == 0)
    def _():
        o_ref[...] = jnp.zeros(o_ref.shape, F32)

    a = jnp.maximum(_dot(h_ref[...], w1_ref[...]), 0.0)
    o_ref[...] += _dot((a * a).astype(BF16), w2_ref[...])

    @pl.when(f == pl.num_programs(2) - 1)
    def _():
        x2 = x1_ref[...] + g2_ref[...] * o_ref[...]
        o_ref[...] = _rms(x2, nf_ref[...]) if final_norm else x2


def _ffn(h2, w1_b, w2_b, x1, mod, normf_g, final_norm):
    B, S, D = x1.shape
    d_ff = w1_b.shape[1]
    tm = min(TM_FFN, S)
    tf = min(TF_FFN, d_ff)
    tok = pl.BlockSpec((None, tm, D), lambda b, i, f: (b, i, 0))
    return pl.pallas_call(
        functools.partial(_ffn_kernel, final_norm=final_norm),
        grid=(B, S // tm, d_ff // tf),
        in_specs=[tok,
                  pl.BlockSpec((D, tf), lambda b, i, f: (0, f)),
                  pl.BlockSpec((tf, D), lambda b, i, f: (f, 0)),
                  tok,
                  pl.BlockSpec((None, None, 1, D), lambda b, i, f: (b, 5, 0, 0)),
                  pl.BlockSpec((1, D), lambda b, i, f: (0, 0))],
        out_specs=tok,
        out_shape=jax.ShapeDtypeStruct((B, S, D), F32),
        compiler_params=_params(("arbitrary", "arbitrary", "arbitrary"), FFN_VMEM_LIMIT_BYTES),
        name="ffn",
    )(h2, w1_b, w2_b, x1, mod, normf_g)


def _rope_tables(seq):
    inv = ROPE_THETA ** (-jnp.arange(0, HEAD_DIM, 2, dtype=F32) / HEAD_DIM)
    ang = jnp.arange(seq, dtype=F32)[:, None] * inv[None, :]
    cos, sin = jnp.cos(ang), jnp.sin(ang)
    return jnp.concatenate([cos, cos], axis=-1), jnp.concatenate([-sin, sin], axis=-1)


def _slc_from_cmp(n_blk):
    sj = np.arange(LANES)[:, None]
    ci = np.arange(n_blk)[None, :]
    m = (ci * CMP_STRIDE <= sj * SLC_LEN + SLC_LEN - 1) & (ci * CMP_STRIDE + CMP_LEN - 1 >= sj * SLC_LEN)
    return jnp.asarray(np.concatenate([m, np.ones((2 * SUBLANES, n_blk), bool)], axis=0), dtype=BF16)


def _cmp_visibility_bias(n_blk, tq):
    d = np.arange(2 * n_blk)[:, None] - n_blk
    r = np.arange(tq)[None, :]
    return jnp.asarray(np.where(CMP_STRIDE * d + CMP_LEN - 1 <= r, 0.0, MASK_BIAS), dtype=F32)


def _block_onehot(seq):
    m = (np.arange(seq)[:, None] // SLC_LEN) == np.arange(LANES)[None, :]
    return jnp.asarray(m, dtype=BF16)


def _prefix_ones():
    return jnp.asarray(np.arange(LANES)[:, None] >= np.arange(LANES)[None, :], dtype=BF16)


def _window_bias(tq):
    key = np.arange(WINDOW + tq)[:, None]
    row = np.arange(tq)[None, :]
    ok = (key > row) & (key <= row + WINDOW)
    return jnp.asarray(np.where(ok, 0.0, MASK_BIAS), dtype=BF16)


def kernel(x, c, w_ada, b_ada, norm1_g, w_in, conv_w, conv_b, cmp_pe_k, cmp_pe_v, cmp_w1_k, cmp_w2_k,
           cmp_w1_v, cmp_w2_v, gnorm_conv_g, gnorm_attn_g, w_out, norm2_g, w_ff1, w_ff2, normf_g):
    B, S, D = x.shape
    depth = w_ada.shape[0]
    assert S % (TILES_PER_STEP * TQ) == 0 and S // SLC_LEN <= LANES and S >= WINDOW + TQ
    assert w_in.shape[2] == D_IN
    cos_f, sin_f = _rope_tables(S)
    s2c = _slc_from_cmp(S // CMP_STRIDE)
    cbias = _cmp_visibility_bias(S // CMP_STRIDE, min(TQ, S))
    onehot = _block_onehot(S)
    tri = _prefix_ones()
    eye = jnp.eye(min(TQ, S), dtype=BF16)
    wbias = _window_bias(min(TQ, S))
    for l in range(depth):
        mod = _adaln(c, w_ada[l], b_ada[l]).reshape(B, 6, 1, D)
        w_in_b = w_in[l].astype(BF16)
        w_gate_b = jnp.pad(w_in[l][:, COL_GATE:], ((0, 0), (0, LANES - N_BRANCH * N_HEADS))).astype(BF16)
        (yconv, q, kc, vc, ksl, vsl, kwn, vwn, gates) = _in_proj(
            x, mod, norm1_g[l][None], w_in_b, w_gate_b, conv_w[l], conv_b[l][None], gnorm_conv_g[l][None], cos_f, sin_f)
        k_cmp = _compress(kc, cmp_pe_k[l], cmp_w1_k[l].astype(BF16), cmp_w2_k[l].astype(BF16))
        v_cmp = _compress(vc, cmp_pe_v[l], cmp_w1_v[l].astype(BF16), cmp_w2_v[l].astype(BF16))
        yattn = _nsa_attention(q, k_cmp, v_cmp, ksl, vsl, kwn, vwn, cbias, s2c, tri, eye, onehot, wbias, gates)
        x1, h2 = _out_proj(yconv, yattn, x, w_out[l].astype(BF16), gnorm_attn_g[l][None], mod, norm2_g[l][None])
        x = _ffn(h2, w_ff1[l].astype(BF16), w_ff2[l].astype(BF16), x1, mod, normf_g[None],
                 final_norm=(l == depth - 1))
    return x
```
